```python
import jax, jax.numpy as jnp
from jax import lax
import numpy as np

D_MODEL = 1024
BATCH = 8
SEQ = 4096
DEPTH = 1

N_META = 16
CONV_WIDTH = D_MODEL
CONV_KERNEL = 31
RWKV_HEAD = 64
RWKV_HEADS = D_MODEL // RWKV_HEAD
RWKV_WIDTH = RWKV_HEADS * RWKV_HEAD
DECAY_LORA = 64
ICLR_LORA = 64
GATE_LORA = 160
D_FF = 2816
FFN_KERNEL = 3
ALPHA = (2.0 * DEPTH) ** 0.25
BETA = (8.0 * DEPTH) ** -0.25
LN_EPS = 1e-5
GN_EPS = 64e-5

N_RWKV_COLS = 3 * RWKV_WIDTH + DECAY_LORA + ICLR_LORA + GATE_LORA
N_IN = 2 * CONV_WIDTH + N_RWKV_COLS + 2 * D_MODEL

kernel_name = "hybrid_conformer_rwkv7_gated_deepnorm"


def layer_norm(x, g, b, eps=LN_EPS):
    xf = x.astype(jnp.float32)
    mu = jnp.mean(xf, axis=-1, keepdims=True)
    xc = xf - mu
    var = jnp.mean(xc * xc, axis=-1, keepdims=True)
    y = xc * lax.rsqrt(var + eps) * g.astype(jnp.float32) + b.astype(jnp.float32)
    return y.astype(x.dtype)


def causal_dwconv(x, w):
    k, c = w.shape
    return lax.conv_general_dilated(
        x, w[:, None, :].astype(x.dtype), window_strides=(1,), padding=[(k - 1, 0)],
        dimension_numbers=("NWC", "WIO", "NWC"), feature_group_count=c)


def token_shift(p):
    return jnp.pad(p, ((0, 0), (1, 0), (0, 0)))[:, :-1]


def rwkv7_recurrence(r, decay, k, v, a, b):
    bsz, _, h, n = r.shape

    def step(s, inp):
        r_t, w_t, k_t, v_t, a_t, b_t = inp
        sa = jnp.einsum("bhvk,bhk->bhv", s, a_t)
        s = s * w_t[:, :, None, :] + sa[..., None] * b_t[:, :, None, :] + v_t[..., None] * k_t[:, :, None, :]
        y = jnp.einsum("bhvk,bhk->bhv", s, r_t)
        return s, y

    xs = tuple(jnp.moveaxis(z, 1, 0) for z in (r, decay, k, v, a, b))
    s0 = jnp.zeros((bsz, h, n, n), jnp.float32)
    _, ys = lax.scan(step, s0, xs)
    return jnp.moveaxis(ys, 0, 1)


def rwkv7_time_mix(p, mu, w0, w_decay_up, a0, a_up, g_up, k_k, k_a, r_k, lnx_g, lnx_b, w_out):
    bsz, t, _ = p.shape
    h, n, wd = RWKV_HEADS, RWKV_HEAD, RWKV_WIDTH
    p = p + (token_shift(p) - p) * mu
    i1, i2, i3 = wd, 2 * wd, 3 * wd
    i4, i5 = i3 + DECAY_LORA, i3 + DECAY_LORA + ICLR_LORA
    r, k, v = p[..., :i1], p[..., i1:i2], p[..., i2:i3]
    xw, xa, xg = p[..., i3:i4], p[..., i4:i5], p[..., i5:]
    w = -jax.nn.softplus(-(w0 + jnp.tanh(xw) @ w_decay_up)) - 0.5
    a = jax.nn.sigmoid(a0 + xa @ a_up)
    g = jax.nn.sigmoid(xg) @ g_up
    heads = lambda z: z.reshape(bsz, t, h, n).astype(jnp.float32)
    kk = heads(k * k_k)
    kk = kk / jnp.maximum(jnp.sqrt(jnp.sum(kk * kk, axis=-1, keepdims=True)), 1e-12)
    k = k * (1.0 + (a - 1.0) * k_a)
    decay = jnp.exp(-jnp.exp(heads(w)))
    rh, kh, vh, ah = heads(r), heads(k), heads(v), heads(a)
    o = rwkv7_recurrence(rh, decay, kh, vh, -kk, kk * ah)
    o = layer_norm(o, lnx_g.reshape(h, n), lnx_b.reshape(h, n), GN_EPS)
    o = o + jnp.sum(rh * kh * r_k.astype(jnp.float32), axis=-1, keepdims=True) * vh
    o = o.reshape(bsz, t, wd).astype(p.dtype) * g
    return o @ w_out


def conformer_conv(p, conv_dw, conv_dw_b, conv_ln_g, conv_ln_b, w_conv_out):
    c = p[..., :CONV_WIDTH] * jax.nn.sigmoid(p[..., CONV_WIDTH:])
    c = causal_dwconv(c, conv_dw) + conv_dw_b
    c = jax.nn.silu(layer_norm(c, conv_ln_g, conv_ln_b))
    return c @ w_conv_out


def conv_gated_mlp(h, ffn_up, ffn_dw, ffn_down):
    u = causal_dwconv(h @ ffn_up, ffn_dw)
    return (jax.nn.silu(u[..., D_FF:]) * u[..., :D_FF]) @ ffn_down


def setup_inputs(seed: int = 0) -> dict:
    key = jax.random.key(seed)
    ks = iter(jax.random.split(key, 40))
    L, D = DEPTH, D_MODEL

    def nrm(shape, scale):
        return scale * jax.random.normal(next(ks), shape, jnp.float32)

    def gain(shape):
        return 1.0 + nrm(shape, 0.02)

    def unif(shape, lo, hi):
        return jax.random.uniform(next(ks), shape, jnp.float32, minval=lo, maxval=hi)

    return {
        "x": nrm((BATCH, SEQ, D), 1.0),
        "meta_tokens": nrm((N_META, D), 1.0),
        "ln_in_g": gain((D,)),
        "ln_in_b": nrm((D,), 0.02),
        "w_in": nrm((L, D, N_IN), D ** -0.5),
        "b_gate": nrm((L, 2 * D), 0.02),
        "conv_dw": nrm((L, CONV_KERNEL, CONV_WIDTH), CONV_KERNEL ** -0.5),
        "conv_dw_b": nrm((L, CONV_WIDTH), 0.02),
        "conv_ln_g": gain((L, CONV_WIDTH)),
        "conv_ln_b": nrm((L, CONV_WIDTH), 0.02),
        "w_conv_out": nrm((L, CONV_WIDTH, D), CONV_WIDTH ** -0.5),
        "rwkv_mu": unif((L, N_RWKV_COLS), 0.1, 0.9),
        "w0": unif((L, RWKV_WIDTH), -6.5, -1.5),
        "w_decay_up": nrm((L, DECAY_LORA, RWKV_WIDTH), 0.1 * DECAY_LORA ** -0.5),
        "a0": nrm((L, RWKV_WIDTH), 0.1),
        "a_up": nrm((L, ICLR_LORA, RWKV_WIDTH), 0.1 * ICLR_LORA ** -0.5),
        "g_up": nrm((L, GATE_LORA, RWKV_WIDTH), GATE_LORA ** -0.5),
        "k_k": 0.85 + nrm((L, RWKV_WIDTH), 0.02),
        "k_a": gain((L, RWKV_WIDTH)),
        "r_k": -0.04 + nrm((L, RWKV_HEADS, RWKV_HEAD), 0.02),
        "lnx_g": gain((L, RWKV_WIDTH)),
        "lnx_b": nrm((L, RWKV_WIDTH), 0.02),
        "w_rwkv_out": nrm((L, RWKV_WIDTH, D), RWKV_WIDTH ** -0.5),
        "w_o": nrm((L, D, D), BETA * D ** -0.5),
        "ln1_g": gain((L, D)),
        "ln1_b": nrm((L, D), 0.02),
        "ffn_up": nrm((L, D, 2 * D_FF), D ** -0.5),
        "ffn_dw": nrm((L, FFN_KERNEL, 2 * D_FF), FFN_KERNEL ** -0.5),
        "ffn_down": nrm((L, D_FF, D), BETA * D_FF ** -0.5),
        "ln2_g": gain((L, D)),
        "ln2_b": nrm((L, D), 0.02),
    }


def reference(x, meta_tokens, ln_in_g, ln_in_b, w_in, b_gate, conv_dw, conv_dw_b, conv_ln_g,
              conv_ln_b, w_conv_out, rwkv_mu, w0, w_decay_up, a0, a_up, g_up, k_k, k_a, r_k,
              lnx_g, lnx_b, w_rwkv_out, w_o, ln1_g, ln1_b, ffn_up, ffn_dw, ffn_down, ln2_g, ln2_b):
    bsz = x.shape[0]
    meta = jnp.broadcast_to(meta_tokens[None].astype(x.dtype), (bsz, N_META, D_MODEL))
    h = layer_norm(jnp.concatenate([meta, x], axis=1), ln_in_g, ln_in_b)
    c_end = 2 * CONV_WIDTH
    r_end = c_end + N_RWKV_COLS
    for l in range(DEPTH):
        p = h @ w_in[l]
        y_conv = conformer_conv(p[..., :c_end], conv_dw[l], conv_dw_b[l], conv_ln_g[l],
                                conv_ln_b[l], w_conv_out[l])
        y_rwkv = rwkv7_time_mix(p[..., c_end:r_end], rwkv_mu[l], w0[l], w_decay_up[l], a0[l],
                                a_up[l], g_up[l], k_k[l], k_a[l], r_k[l], lnx_g[l], lnx_b[l],
                                w_rwkv_out[l])
        gates = jax.nn.sigmoid(p[..., r_end:] + b_gate[l])
        y = gates[..., :D_MODEL] * y_conv + gates[..., D_MODEL:] * y_rwkv
        h = layer_norm(ALPHA * h + y @ w_o[l], ln1_g[l], ln1_b[l])
        h = layer_norm(ALPHA * h + conv_gated_mlp(h, ffn_up[l], ffn_dw[l], ffn_down[l]),
                       ln2_g[l], ln2_b[l])
    return h[:, N_META:]
```

```python
import functools
import math

import jax
import jax.numpy as jnp
import numpy as np
from jax import lax
from jax.experimental import pallas as pl
from jax.experimental.pallas import tpu as pltpu

D_MODEL = 1024
N_META = 16
CONV_KERNEL = 31
RWKV_HEAD = 64
RWKV_HEADS = D_MODEL // RWKV_HEAD
DECAY_LORA = 64
ICLR_LORA = 64
GATE_LORA = 160
D_FF = 2816
FFN_KERNEL = 3
LN_EPS = 1e-5
GN_EPS = 64e-5

FRONT = 256
TT_PREP = 128
TT = 256
CHUNK = 64
PAIR = 2 * RWKV_HEAD
N_PAIR = D_MODEL // PAIR
HALO = 32
RB = 16
X_COLS = 384
VMEM_LIMIT = 56 * 1024 * 1024

F32 = jnp.float32
BF16 = jnp.bfloat16
HI = lax.Precision.HIGHEST


def _dot(a, b, **kw):
    return jnp.dot(a, b, preferred_element_type=F32, **kw)


def _dot_nt(a, b):
    return lax.dot_general(a, b, (((1,), (1,)), ((), ())), preferred_element_type=F32)


def _dot_tn(a, b, **kw):
    return lax.dot_general(a, b, (((0,), (0,)), ((), ())), preferred_element_type=F32, **kw)


def _sigmoid(x):
    return 1.0 / (1.0 + jnp.exp(-x))


def _layer_norm(x, g, b, eps):
    mu = jnp.mean(x, axis=-1, keepdims=True)
    xc = x - mu
    var = jnp.mean(xc * xc, axis=-1, keepdims=True)
    return xc * lax.rsqrt(var + eps) * g + b


def _row_loop(n_rows, rb, fn):
    def body(i, c):
        fn(pl.multiple_of(i * rb, rb))
        return c
    lax.fori_loop(0, n_rows // rb, body, 0)


def _seg_sum(xb, g_ref):
    parts = [_dot(xb[:, c * 256:(c + 1) * 256], g_ref[...]) for c in range(D_MODEL // 256)]
    return jnp.concatenate(parts, axis=1)


def _mix_prep_body(x_ref, meta_ref, lng_ref, lnb_ref, wc_ref, wrkv_ref, wx_ref, wgt_ref, bgate_ref,
                   convw_ref, convb_ref, clng_ref, clnb_ref, wco_ref, mu_rkv_ref, mu_x_ref,
                   w0_ref, wdu_ref, a0_ref, wau_ref, wgu_ref, kk_ref, ka_ref, gones_ref,
                   h_ref, r_ref, ld_ref, k2_ref, v_ref, kkn_ref, b_ref, g_ref, ycg_ref, gb_ref,
                   hb_s, pc_s, cbuf_s, cact_s, pbuf_s, t1_s, x1_s, s2_s, z1_s, z2_s, kk2_s, *, tt):
    ti = pl.program_id(1)
    n_front = FRONT // tt
    first_real = FRONT - N_META

    @pl.when(ti == 0)
    def _():
        cbuf_s[0:HALO, :] = jnp.zeros((HALO, D_MODEL), F32)
        pbuf_s[0:8, :] = jnp.zeros((8, pbuf_s.shape[1]), F32)

    def ln_rows(base):
        rows = pl.ds(base, RB)
        xin = jnp.where(ti < n_front, meta_ref[rows, :], x_ref[0, rows, :])
        h = _layer_norm(xin, lng_ref[...], lnb_ref[...], LN_EPS)
        grow = ti * tt + base + lax.broadcasted_iota(jnp.int32, (RB, 1), 0)
        h = jnp.where(grow >= first_real, h, 0.0)
        h_ref[0, rows, :] = h
        hb_s[rows, :] = h.astype(BF16)
    _row_loop(tt, RB, ln_rows)

    pc_s[:, 0:2 * D_MODEL] = _dot(hb_s[...], wc_ref[...])

    def glu_rows(base):
        rows = pl.ds(base, RB)
        c = pc_s[rows, 0:D_MODEL] * _sigmoid(pc_s[rows, D_MODEL:2 * D_MODEL])
        cbuf_s[pl.ds(base + HALO, RB), :] = c
    _row_loop(tt, RB, glu_rows)

    cr = 64
    def conv_rows(base):
        for cb in range(D_MODEL // 128):
            cs = slice(cb * 128, (cb + 1) * 128)
            win = cbuf_s[pl.ds(base, cr + HALO), cs]
            acc = jnp.zeros((cr, 128), F32) + convb_ref[:, cs]
            for r in range(8):
                sh = win if r == 0 else pltpu.roll(win, r, axis=0)
                for q in range(4):
                    d = 8 * q + r
                    if d > CONV_KERNEL - 1:
                        continue
                    j = CONV_KERNEL - 1 - d
                    acc = acc + convw_ref[j:j + 1, cs] * sh[HALO - 8 * q:HALO - 8 * q + cr]
            pc_s[pl.ds(base, cr), cs] = acc
    _row_loop(tt, cr, conv_rows)

    cbuf_s[0:HALO, :] = cbuf_s[tt:tt + HALO, :]

    def cln_rows(base):
        rows = pl.ds(base, RB)
        y = _layer_norm(pc_s[rows, 0:D_MODEL], clng_ref[...], clnb_ref[...], LN_EPS)
        cact_s[rows, :] = (y * _sigmoid(y)).astype(BF16)
    _row_loop(tt, RB, cln_rows)

    yc = _dot(cact_s[...], wco_ref[...])
    pg = _dot(hb_s[...], wgt_ref[...]) + bgate_ref[...]
    ycg_ref[0] = _sigmoid(pg[:, 0:D_MODEL]) * yc
    gb_ref[0] = _sigmoid(pg[:, D_MODEL:2 * D_MODEL])

    n_rkv = 3 * D_MODEL
    pbuf_s[8:8 + tt, 0:n_rkv] = _dot(hb_s[...], wrkv_ref[...])
    pbuf_s[8:8 + tt, n_rkv:n_rkv + X_COLS] = _dot(hb_s[...], wx_ref[...])

    def lerp_rows(base):
        rows = pl.ds(base, RB)

        def lerp(cols, mu):
            win = pbuf_s[pl.ds(base, RB + 8), cols]
            cur = win[8:8 + RB]
            prev = pltpu.roll(win, 1, axis=0)[8:8 + RB]
            return cur + (prev - cur) * mu

        for c in range(3 * D_MODEL // 256):
            cols = slice(c * 256, (c + 1) * 256)
            pc_s[rows, cols] = lerp(cols, mu_rkv_ref[:, cols])
        x1 = lerp(slice(n_rkv, n_rkv + 128), mu_x_ref[:, 0:128])
        x2 = lerp(slice(n_rkv + 128, n_rkv + X_COLS), mu_x_ref[:, 128:X_COLS])
        t1_s[rows, :] = jnp.tanh(x1).astype(BF16)
        x1_s[rows, :] = x1.astype(BF16)
        s2_s[rows, :] = _sigmoid(x2).astype(BF16)
    _row_loop(tt, RB, lerp_rows)

    pbuf_s[0:8, :] = pbuf_s[tt:tt + 8, :]

    z1_s[...] = _dot(t1_s[...], wdu_ref[...])
    z2_s[...] = _dot(x1_s[...], wau_ref[...])
    g_ref[0] = _dot(s2_s[...], wgu_ref[...])

    def kk_rows(base):
        rows = pl.ds(base, RB)
        kk = pc_s[rows, D_MODEL:2 * D_MODEL] * kk_ref[...]
        kk2_s[rows, :] = (kk * kk).astype(BF16)
    _row_loop(tt, RB, kk_rows)

    ss = _seg_sum(kk2_s[...], gones_ref)
    pbuf_s[8:8 + tt, 0:D_MODEL] = ss

    def out_rows(base):
        rows = pl.ds(base, RB)
        r = pc_s[rows, 0:D_MODEL]
        k = pc_s[rows, D_MODEL:2 * D_MODEL]
        v = pc_s[rows, 2 * D_MODEL:3 * D_MODEL]
        ld = -math.exp(-0.5) * _sigmoid(w0_ref[...] + z1_s[rows, :])
        a = _sigmoid(a0_ref[...] + z2_s[rows, :])
        ssr = pbuf_s[pl.ds(base + 8, RB), 0:D_MODEL]
        kkn = (k * kk_ref[...]) * jnp.minimum(lax.rsqrt(ssr), 1e12)
        r_ref[0, rows, :] = r
        v_ref[0, rows, :] = v
        ld_ref[0, rows, :] = ld
        k2_ref[0, rows, :] = k * (1.0 + (a - 1.0) * ka_ref[...])
        kkn_ref[0, rows, :] = kkn
        b_ref[0, rows, :] = kkn * a
    _row_loop(tt, RB, out_rows)


def _block_diag(yb, bdmask):
    y2 = jnp.concatenate([yb, yb], axis=0)
    return jnp.where(bdmask, y2, jnp.zeros_like(y2))


def _pair_chunk(ldc, rc, kc, vc, kknc, bc, m_bd, consts):
    ltri, eye2, trimask, bdmask, bdmask32, lane0, ones_c = consts
    cum = _dot(ltri, ldc, precision=HI)
    cum_last = cum[CHUNK - 1:CHUNK, :]
    w_incl = jnp.exp(cum)
    w_excl = jnp.exp(cum - ldc)
    w_inv = jnp.exp(-cum)
    w_rest = jnp.exp(cum_last - cum)
    at = -(kknc * w_excl)
    bt = (bc * w_inv).astype(BF16)
    kt = (kc * w_inv).astype(BF16)
    rt = rc * w_incl
    zero = jnp.zeros_like(bt)
    ys = jnp.concatenate([jnp.where(lane0, bt, zero), jnp.where(lane0, zero, bt),
                          jnp.where(lane0, kt, zero), jnp.where(lane0, zero, kt)], axis=0)
    atb = at.astype(BF16)
    rtb = rt.astype(BF16)
    sc = _dot_nt(jnp.concatenate([atb, rtb], axis=0), ys)
    sc = jnp.where(trimask, sc, 0.0)
    a_ab = sc[0:CHUNK, 0:PAIR]
    a_ak = sc[0:CHUNK, PAIR:2 * PAIR]
    r_b = sc[CHUNK:2 * CHUNK, 0:PAIR]
    r_k = sc[CHUNK:2 * CHUNK, PAIR:2 * PAIR]

    n = _dot(a_ab.astype(BF16), _block_diag(a_ab.astype(BF16), bdmask))
    t = eye2 + a_ab
    for _ in range(4):
        res = _dot(jnp.concatenate([n, t], axis=0).astype(BF16), _block_diag(n.astype(BF16), bdmask))
        n = res[0:CHUNK]
        t = t + res[CHUNK:2 * CHUNK]
    t = t + _dot(t.astype(BF16), _block_diag(n.astype(BF16), bdmask))

    mb = m_bd.astype(BF16)
    vb = vc.astype(BF16)
    v_bd = _block_diag(vb, bdmask)
    q = _dot(jnp.concatenate([atb, a_ak.astype(BF16)], axis=1), jnp.concatenate([mb, v_bd], axis=0))
    u = _dot(t.astype(BF16), _block_diag(q.astype(BF16), bdmask))
    ub = u.astype(BF16)
    y = _dot(jnp.concatenate([rtb, r_b.astype(BF16), r_k.astype(BF16)], axis=1),
             jnp.concatenate([mb, _block_diag(ub, bdmask), v_bd], axis=0))
    wl_col = jnp.exp(_dot_tn(ldc, ones_c, precision=HI))
    bw = (bc * w_rest).astype(BF16)
    kw = (kc * w_rest).astype(BF16)
    upd = _dot_tn(jnp.concatenate([bw, kw], axis=0), jnp.concatenate([ub, vb], axis=0))
    m_new = wl_col * m_bd + jnp.where(bdmask32, upd, 0.0)
    return y, m_new


def _mix_core_body(r_ref, ld_ref, k2_ref, v_ref, kkn_ref, b_ref, g_ref, ycg_ref, gb_ref, h_ref,
                   ltri_ref, eye2_ref, trimask_ref, bdmask_ref, bdmask32_ref, lane0_ref, gones_ref, gmean_ref,
                   rk_ref, lnxg_ref, lnxb_ref, wro_ref, wo_ref, ln1g_ref, ln1b_ref,
                   h1_ref, m_s, o_s, ob_s, t32_s, *, tt, alpha):
    ti = pl.program_id(1)

    @pl.when(ti == 0)
    def _():
        m_s[...] = jnp.zeros(m_s.shape, F32)

    consts = (ltri_ref[...], eye2_ref[...], trimask_ref[...] > 0.5, bdmask_ref[...] > 0.5,
              bdmask32_ref[...] > 0.5, lane0_ref[...] > 0.5, jnp.ones((CHUNK, PAIR), F32))

    def chunk_body(c, carry):
        rows = pl.ds(pl.multiple_of(c * CHUNK, CHUNK), CHUNK)
        for p in range(N_PAIR):
            cols = slice(p * PAIR, (p + 1) * PAIR)
            y, m_new = _pair_chunk(ld_ref[0, rows, cols], r_ref[0, rows, cols], k2_ref[0, rows, cols],
                                   v_ref[0, rows, cols], kkn_ref[0, rows, cols], b_ref[0, rows, cols],
                                   m_s[p], consts)
            m_s[p] = m_new
            o_s[rows, cols] = y
        return carry
    lax.fori_loop(0, tt // CHUNK, chunk_body, 0)

    def ob_rows(base):
        rows = pl.ds(base, RB)
        ob_s[rows, :] = o_s[rows, :].astype(BF16)
    _row_loop(tt, RB, ob_rows)
    t32_s[...] = _seg_sum(ob_s[...], gmean_ref)

    def xc_rows(base):
        rows = pl.ds(base, RB)
        xc = o_s[rows, :] - t32_s[rows, :]
        o_s[rows, :] = xc
        ob_s[rows, :] = (xc * xc).astype(BF16)
    _row_loop(tt, RB, xc_rows)
    t32_s[...] = _seg_sum(ob_s[...], gmean_ref)

    def on_rows(base):
        rows = pl.ds(base, RB)
        on = o_s[rows, :] * lax.rsqrt(t32_s[rows, :] + GN_EPS) * lnxg_ref[...] + lnxb_ref[...]
        o_s[rows, :] = on
        ob_s[rows, :] = (r_ref[0, rows, :] * k2_ref[0, rows, :] * rk_ref[...]).astype(BF16)
    _row_loop(tt, RB, on_rows)
    t32_s[...] = _seg_sum(ob_s[...], gones_ref)

    def o2_rows(base):
        rows = pl.ds(base, RB)
        o2 = (o_s[rows, :] + t32_s[rows, :] * v_ref[0, rows, :]) * g_ref[0, rows, :]
        ob_s[rows, :] = o2.astype(BF16)
    _row_loop(tt, RB, o2_rows)

    y = ycg_ref[0] + gb_ref[0] * _dot(ob_s[...], wro_ref[...])
    t32_s[...] = _dot(y.astype(BF16), wo_ref[...])

    def ln1_rows(base):
        rows = pl.ds(base, RB)
        h1_ref[0, rows, :] = _layer_norm(alpha * h_ref[0, rows, :] + t32_s[rows, :],
                                         ln1g_ref[...], ln1b_ref[...], LN_EPS)
    _row_loop(tt, RB, ln1_rows)


def _ffn_body(h1_ref, wup_ref, dw_ref, wdn_ref, ln2g_ref, ln2b_ref, out_ref,
              hb_s, u_s, act_s, y_s, *, tt, alpha):
    ti = pl.program_id(1)
    first_real = FRONT - N_META

    @pl.when(ti == 0)
    def _():
        u_s[0:8, :] = jnp.zeros((8, 2 * D_FF), F32)

    def hb_rows(base):
        rows = pl.ds(base, RB)
        grow = ti * tt + base + lax.broadcasted_iota(jnp.int32, (RB, 1), 0)
        hb_s[rows, :] = jnp.where(grow >= first_real, h1_ref[0, rows, :], 0.0).astype(BF16)
    _row_loop(tt, RB, hb_rows)

    u_s[8:8 + tt, :] = _dot(hb_s[...], wup_ref[...])

    def conv_rows(base):
        rows = pl.ds(base, RB)

        def conv(cols):
            win = u_s[pl.ds(base, RB + 8), cols]
            acc = dw_ref[2:3, cols] * win[8:8 + RB]
            for d in (1, 2):
                acc = acc + dw_ref[2 - d:3 - d, cols] * pltpu.roll(win, d, axis=0)[8:8 + RB]
            return acc

        for c in range(D_FF // 256):
            lin = conv(slice(c * 256, (c + 1) * 256))
            gate = conv(slice(D_FF + c * 256, D_FF + (c + 1) * 256))
            act_s[rows, c * 256:(c + 1) * 256] = (gate * _sigmoid(gate) * lin).astype(BF16)
    _row_loop(tt, RB, conv_rows)

    u_s[0:8, :] = u_s[tt:tt + 8, :]
    y_s[...] = _dot(act_s[...], wdn_ref[...])

    def ln2_rows(base):
        rows = pl.ds(base, RB)
        out_ref[0, rows, :] = _layer_norm(alpha * h1_ref[0, rows, :] + y_s[rows, :],
                                          ln2g_ref[...], ln2b_ref[...], LN_EPS)
    _row_loop(tt, RB, ln2_rows)


def _const(shape):
    nd = len(shape)
    return pl.BlockSpec(shape, lambda b, t: (0,) * nd, pipeline_mode=pl.Buffered(1))


def _row2d(v):
    return v.reshape(1, -1).astype(F32)


def _mask_constants():
    t = np.arange(2 * CHUNK)[:, None] % CHUNK
    s = np.arange(2 * PAIR)[None, :] % RWKV_HEAD
    strict = (t > s)
    incl = (t >= s)
    trimask = np.where(np.arange(2 * CHUNK)[:, None] < CHUNK, strict, incl).astype(np.float32)
    hh = np.arange(PAIR) // RWKV_HEAD
    bdmask = (hh[:, None] == hh[None, :]).astype(np.float32)
    ltri = np.tril(np.ones((CHUNK, CHUNK), np.float32))
    eye2 = np.concatenate([np.eye(CHUNK, dtype=np.float32)] * 2, axis=1)
    lane0 = np.broadcast_to((np.arange(PAIR) < RWKV_HEAD)[None, :], (CHUNK, PAIR)).astype(np.float32)
    h4 = np.arange(256) // RWKV_HEAD
    gones = (h4[:, None] == h4[None, :]).astype(np.float32)
    return dict(ltri=jnp.asarray(ltri), eye2=jnp.asarray(eye2), trimask=jnp.asarray(trimask),
                bdmask=jnp.asarray(bdmask, BF16), bdmask32=jnp.asarray(bdmask), lane0=jnp.asarray(lane0, BF16),
                gones=jnp.asarray(gones, BF16), gmean=jnp.asarray(gones / RWKV_HEAD, BF16))


def _layer(x, prm, consts, bsz, n_tiles_real, alpha):
    d = D_MODEL
    tp = FRONT + n_tiles_real * TT
    c_end = 2 * d
    i_r = c_end
    i_x = i_r + 3 * d
    r_end = i_x + DECAY_LORA + ICLR_LORA + GATE_LORA
    w_in = prm["w_in"]
    wc = w_in[:, :c_end].astype(BF16)
    wrkv = w_in[:, i_r:i_x].astype(BF16)
    wx = jnp.pad(w_in[:, i_x:r_end], ((0, 0), (0, X_COLS - (r_end - i_x)))).astype(BF16)
    wgt = w_in[:, r_end:].astype(BF16)
    mu = prm["rwkv_mu"]
    mu_rkv = _row2d(mu[:3 * d])
    mu_x = _row2d(jnp.pad(mu[3 * d:], (0, X_COLS - (r_end - i_x))))
    wdu = jnp.pad(prm["w_decay_up"], ((0, 128 - DECAY_LORA), (0, 0))).astype(BF16)
    wau = jnp.pad(prm["a_up"], ((DECAY_LORA, 0), (0, 0))).astype(BF16)
    wgu = jnp.pad(prm["g_up"], ((0, 256 - GATE_LORA), (0, 0))).astype(BF16)

    tt = TT_PREP
    n_front = FRONT // tt
    n_t = tp // tt
    tile = pl.BlockSpec((1, tt, d), lambda b, t: (b, t, 0))
    in_specs = [
        pl.BlockSpec((1, tt, d), lambda b, t: (b, jnp.maximum(t - n_front, 0), 0)),
        pl.BlockSpec((tt, d), lambda b, t: (jnp.minimum(t, n_front - 1), 0)),
        _const((1, d)), _const((1, d)),
        _const((d, 2 * d)), _const((d, 3 * d)), _const((d, X_COLS)), _const((d, 2 * d)), _const((1, 2 * d)),
        _const((CONV_KERNEL, d)), _const((1, d)), _const((1, d)), _const((1, d)), _const((d, d)),
        _const((1, 3 * d)), _const((1, X_COLS)),
        _const((1, d)), _const((128, d)), _const((1, d)), _const((128, d)), _const((256, d)),
        _const((1, d)), _const((1, d)), _const((256, 256)),
    ]
    act = jax.ShapeDtypeStruct((bsz, tp, d), F32)
    scratch = [
        pltpu.VMEM((tt, d), BF16),
        pltpu.VMEM((tt, 3 * d), F32),
        pltpu.VMEM((tt + HALO, d), F32),
        pltpu.VMEM((tt, d), BF16),
        pltpu.VMEM((tt + 8, 3 * d + X_COLS), F32),
        pltpu.VMEM((tt, 128), BF16), pltpu.VMEM((tt, 128), BF16), pltpu.VMEM((tt, 256), BF16),
        pltpu.VMEM((tt, d), F32), pltpu.VMEM((tt, d), F32), pltpu.VMEM((tt, d), BF16),
    ]
    outs = pl.pallas_call(
        functools.partial(_mix_prep_body, tt=tt),
        grid=(bsz, n_t), in_specs=in_specs, out_specs=[tile] * 10, out_shape=[act] * 10,
        scratch_shapes=scratch, name="mix_prep",
        compiler_params=pltpu.CompilerParams(dimension_semantics=("arbitrary", "arbitrary"),
                                             vmem_limit_bytes=VMEM_LIMIT),
    )(x, prm["meta_pad"], prm["ln_in_g"], prm["ln_in_b"], wc, wrkv, wx, wgt, _row2d(prm["b_gate"]),
      prm["conv_dw"].astype(F32), _row2d(prm["conv_dw_b"]), _row2d(prm["conv_ln_g"]), _row2d(prm["conv_ln_b"]),
      prm["w_conv_out"].astype(BF16), mu_rkv, mu_x,
      _row2d(prm["w0"]), wdu, _row2d(prm["a0"]), wau, wgu, _row2d(prm["k_k"]), _row2d(prm["k_a"]),
      consts["gones"])
    h, r, ld, k2, v, kkn, b, g, ycg, gb = outs

    tt = TT
    n_t = tp // tt
    tile = pl.BlockSpec((1, tt, d), lambda b, t: (b, t, 0))
    in_specs = [tile] * 10 + [
        _const((CHUNK, CHUNK)), _const((CHUNK, PAIR)), _const((2 * CHUNK, 2 * PAIR)), _const((PAIR, PAIR)),
        _const((PAIR, PAIR)),
        _const((CHUNK, PAIR)), _const((256, 256)), _const((256, 256)),
        _const((1, d)), _const((1, d)), _const((1, d)), _const((d, d)), _const((d, d)), _const((1, d)), _const((1, d)),
    ]
    scratch = [
        pltpu.VMEM((N_PAIR, PAIR, PAIR), F32),
        pltpu.VMEM((tt, d), F32), pltpu.VMEM((tt, d), BF16), pltpu.VMEM((tt, d), F32),
    ]
    h1 = pl.pallas_call(
        functools.partial(_mix_core_body, tt=tt, alpha=alpha),
        grid=(bsz, n_t), in_specs=in_specs, out_specs=tile, out_shape=act,
        scratch_shapes=scratch, name="mix_core",
        compiler_params=pltpu.CompilerParams(dimension_semantics=("arbitrary", "arbitrary"),
                                             vmem_limit_bytes=VMEM_LIMIT),
    )(r, ld, k2, v, kkn, b, g, ycg, gb, h,
      consts["ltri"], consts["eye2"], consts["trimask"], consts["bdmask"], consts["bdmask32"], consts["lane0"],
      consts["gones"], consts["gmean"],
      _row2d(prm["r_k"]), _row2d(prm["lnx_g"]), _row2d(prm["lnx_b"]),
      prm["w_rwkv_out"].astype(BF16), prm["w_o"].astype(BF16), _row2d(prm["ln1_g"]), _row2d(prm["ln1_b"]))

    n_front = FRONT // tt
    in_specs = [tile, _const((d, 2 * D_FF)), _const((FFN_KERNEL, 2 * D_FF)), _const((D_FF, d)),
                _const((1, d)), _const((1, d))]
    scratch = [pltpu.VMEM((tt, d), BF16), pltpu.VMEM((tt + 8, 2 * D_FF), F32),
               pltpu.VMEM((tt, D_FF), BF16), pltpu.VMEM((tt, d), F32)]
    out = pl.pallas_call(
        functools.partial(_ffn_body, tt=tt, alpha=alpha),
        grid=(bsz, n_t), in_specs=in_specs,
        out_specs=pl.BlockSpec((1, tt, d), lambda b, t: (b, jnp.maximum(t - n_front, 0), 0)),
        out_shape=jax.ShapeDtypeStruct((bsz, n_tiles_real * tt, d), F32),
        scratch_shapes=scratch, name="ffn",
        compiler_params=pltpu.CompilerParams(dimension_semantics=("arbitrary", "arbitrary"),
                                             vmem_limit_bytes=VMEM_LIMIT),
    )(h1, prm["ffn_up"].astype(BF16), prm["ffn_dw"].astype(F32), prm["ffn_down"].astype(BF16),
      _row2d(prm["ln2_g"]), _row2d(prm["ln2_b"]))
    return out


def kernel(x, meta_tokens, ln_in_g, ln_in_b, w_in, b_gate, conv_dw, conv_dw_b, conv_ln_g, conv_ln_b, w_conv_out, rwkv_mu, w0, w_decay_up, a0, a_up, g_up, k_k, k_a, r_k, lnx_g, lnx_b, w_rwkv_out, w_o, ln1_g, ln1_b, ffn_up, ffn_dw, ffn_down, ln2_g, ln2_b):
    bsz, seq, d = x.shape
    depth = w_in.shape[0]
    assert d == D_MODEL and seq % TT == 0 and depth == 1, (x.shape, w_in.shape)
    alpha = (2.0 * depth) ** 0.25
    consts = _mask_constants()
    meta_pad = jnp.concatenate([jnp.zeros((FRONT - N_META, d), F32), meta_tokens.astype(F32)], axis=0)
    prm = dict(meta_pad=meta_pad, ln_in_g=_row2d(ln_in_g), ln_in_b=_row2d(ln_in_b),
               w_in=w_in[0], b_gate=b_gate[0], conv_dw=conv_dw[0], conv_dw_b=conv_dw_b[0],
               conv_ln_g=conv_ln_g[0], conv_ln_b=conv_ln_b[0], w_conv_out=w_conv_out[0], rwkv_mu=rwkv_mu[0],
               w0=w0[0], w_decay_up=w_decay_up[0], a0=a0[0], a_up=a_up[0], g_up=g_up[0], k_k=k_k[0], k_a=k_a[0],
               r_k=r_k[0], lnx_g=lnx_g[0], lnx_b=lnx_b[0], w_rwkv_out=w_rwkv_out[0], w_o=w_o[0],
               ln1_g=ln1_g[0], ln1_b=ln1_b[0], ffn_up=ffn_up[0], ffn_dw=ffn_dw[0], ffn_down=ffn_down[0],
               ln2_g=ln2_g[0], ln2_b=ln2_b[0])
    return _layer(x.astype(F32), prm, consts, bsz, seq // TT, alpha)
```

```python
import functools
import math

import jax
import jax.numpy as jnp
import numpy as np
from jax import lax
from jax.experimental import pallas as pl
from jax.experimental.pallas import tpu as pltpu

D_MODEL = 1024
N_META = 16
CONV_KERNEL = 31
RWKV_HEAD = 64
RWKV_HEADS = D_MODEL // RWKV_HEAD
DECAY_LORA = 64
ICLR_LORA = 64
GATE_LORA = 160
D_FF = 2816
FFN_KERNEL = 3
LN_EPS = 1e-5
GN_EPS = 64e-5

FRONT = 256
TT_PREP = 128
TT = 256
CHUNK = 64
PAIR = 2 * RWKV_HEAD
N_PAIR = D_MODEL // PAIR
HALO = 32
RB = 16
X_COLS = 384
VMEM_LIMIT = 56 * 1024 * 1024

F32 = jnp.float32
BF16 = jnp.bfloat16
HI = lax.Precision.HIGHEST


def _dot(a, b, **kw):
    return jnp.dot(a, b, preferred_element_type=F32, **kw)


def _dot_nt(a, b):
    return lax.dot_general(a, b, (((1,), (1,)), ((), ())), preferred_element_type=F32)


def _dot_tn(a, b, **kw):
    return lax.dot_general(a, b, (((0,), (0,)), ((), ())), preferred_element_type=F32, **kw)


def _sigmoid(x):
    return 1.0 / (1.0 + jnp.exp(-x))


def _layer_norm(x, g, b, eps):
    mu = jnp.mean(x, axis=-1, keepdims=True)
    xc = x - mu
    var = jnp.mean(xc * xc, axis=-1, keepdims=True)
    return xc * lax.rsqrt(var + eps) * g + b


def _row_loop(n_rows, rb, fn, unroll=1):
    def body(i, c):
        fn(pl.multiple_of(i * rb, rb))
        return c
    lax.fori_loop(0, n_rows // rb, body, 0, unroll=unroll)


def _seg_sum(xb, g_ref):
    parts = [_dot(xb[:, c * 256:(c + 1) * 256], g_ref[...]) for c in range(D_MODEL // 256)]
    return jnp.concatenate(parts, axis=1)


def _mix_prep_body(x_ref, meta_ref, lng_ref, lnb_ref, wc_ref, wrkv_ref, wx_ref, wgt_ref, bgate_ref,
                   convw_ref, convb_ref, clng_ref, clnb_ref, wco_ref, mu_rkv_ref, mu_x_ref,
                   w0_ref, wdu_ref, a0_ref, wau_ref, wgu_ref, kk_ref, ka_ref, gones_ref,
                   h_ref, r_ref, ld_ref, k2_ref, v_ref, kkn_ref, b_ref, g_ref, ycg_ref, gb_ref,
                   hb_s, pc_s, cbuf_s, cact_s, pbuf_s, t1_s, x1_s, s2_s, z1_s, z2_s, kk2_s, *, tt):
    ti = pl.program_id(1)
    n_front = FRONT // tt
    first_real = FRONT - N_META

    @pl.when(ti == 0)
    def _():
        cbuf_s[0:HALO, :] = jnp.zeros((HALO, D_MODEL), F32)
        pbuf_s[0:8, :] = jnp.zeros((8, pbuf_s.shape[1]), F32)

    def ln_rows(base):
        rows = pl.ds(base, RB)
        xin = jnp.where(ti < n_front, meta_ref[rows, :], x_ref[0, rows, :])
        h = _layer_norm(xin, lng_ref[...], lnb_ref[...], LN_EPS)
        grow = ti * tt + base + lax.broadcasted_iota(jnp.int32, (RB, 1), 0)
        h = jnp.where(grow >= first_real, h, 0.0)
        h_ref[0, rows, :] = h
        hb_s[rows, :] = h.astype(BF16)
    _row_loop(tt, RB, ln_rows, unroll=4)

    pc_s[:, 0:2 * D_MODEL] = _dot(hb_s[...], wc_ref[...])

    def glu_rows(base):
        rows = pl.ds(base, RB)
        c = pc_s[rows, 0:D_MODEL] * _sigmoid(pc_s[rows, D_MODEL:2 * D_MODEL])
        cbuf_s[pl.ds(base + HALO, RB), :] = c
    _row_loop(tt, RB, glu_rows)

    cr = 64
    def conv_rows(base):
        for cb in range(D_MODEL // 128):
            cs = slice(cb * 128, (cb + 1) * 128)
            win = cbuf_s[pl.ds(base, cr + HALO), cs]
            acc = jnp.zeros((cr, 128), F32) + convb_ref[:, cs]
            for r in range(8):
                sh = win if r == 0 else pltpu.roll(win, r, axis=0)
                for q in range(4):
                    d = 8 * q + r
                    if d > CONV_KERNEL - 1:
                        continue
                    j = CONV_KERNEL - 1 - d
                    acc = acc + convw_ref[j:j + 1, cs] * sh[HALO - 8 * q:HALO - 8 * q + cr]
            pc_s[pl.ds(base, cr), cs] = acc
    _row_loop(tt, cr, conv_rows)

    cbuf_s[0:HALO, :] = cbuf_s[tt:tt + HALO, :]

    def cln_rows(base):
        rows = pl.ds(base, RB)
        y = _layer_norm(pc_s[rows, 0:D_MODEL], clng_ref[...], clnb_ref[...], LN_EPS)
        cact_s[rows, :] = (y * _sigmoid(y)).astype(BF16)
    _row_loop(tt, RB, cln_rows, unroll=4)

    yc = _dot(cact_s[...], wco_ref[...])
    pg = _dot(hb_s[...], wgt_ref[...]) + bgate_ref[...]
    ycg_ref[0] = _sigmoid(pg[:, 0:D_MODEL]) * yc
    gb_ref[0] = _sigmoid(pg[:, D_MODEL:2 * D_MODEL])

    n_rkv = 3 * D_MODEL
    pbuf_s[8:8 + tt, 0:n_rkv] = _dot(hb_s[...], wrkv_ref[...])
    pbuf_s[8:8 + tt, n_rkv:n_rkv + X_COLS] = _dot(hb_s[...], wx_ref[...])

    def lerp_rows(base):
        rows = pl.ds(base, RB)

        def lerp(cols, mu):
            win = pbuf_s[pl.ds(base, RB + 8), cols]
            cur = win[8:8 + RB]
            prev = pltpu.roll(win, 1, axis=0)[8:8 + RB]
            return cur + (prev - cur) * mu

        for c in range(3 * D_MODEL // 256):
            cols = slice(c * 256, (c + 1) * 256)
            pc_s[rows, cols] = lerp(cols, mu_rkv_ref[:, cols])
        x1 = lerp(slice(n_rkv, n_rkv + 128), mu_x_ref[:, 0:128])
        x2 = lerp(slice(n_rkv + 128, n_rkv + X_COLS), mu_x_ref[:, 128:X_COLS])
        t1_s[rows, :] = jnp.tanh(x1).astype(BF16)
        x1_s[rows, :] = x1.astype(BF16)
        s2_s[rows, :] = _sigmoid(x2).astype(BF16)
    _row_loop(tt, RB, lerp_rows)

    pbuf_s[0:8, :] = pbuf_s[tt:tt + 8, :]

    z1_s[...] = _dot(t1_s[...], wdu_ref[...])
    z2_s[...] = _dot(x1_s[...], wau_ref[...])
    g_ref[0] = _dot(s2_s[...], wgu_ref[...])

    def kk_rows(base):
        rows = pl.ds(base, RB)
        kk = pc_s[rows, D_MODEL:2 * D_MODEL] * kk_ref[...]
        kk2_s[rows, :] = (kk * kk).astype(BF16)
    _row_loop(tt, RB, kk_rows)

    ss = _seg_sum(kk2_s[...], gones_ref)
    pbuf_s[8:8 + tt, 0:D_MODEL] = ss

    def out_rows(base):
        rows = pl.ds(base, RB)
        r = pc_s[rows, 0:D_MODEL]
        k = pc_s[rows, D_MODEL:2 * D_MODEL]
        v = pc_s[rows, 2 * D_MODEL:3 * D_MODEL]
        ld = -math.exp(-0.5) * _sigmoid(w0_ref[...] + z1_s[rows, :])
        a = _sigmoid(a0_ref[...] + z2_s[rows, :])
        ssr = pbuf_s[pl.ds(base + 8, RB), 0:D_MODEL]
        kkn = (k * kk_ref[...]) * jnp.minimum(lax.rsqrt(ssr), 1e12)
        r_ref[0, rows, :] = r
        v_ref[0, rows, :] = v
        ld_ref[0, rows, :] = ld
        k2_ref[0, rows, :] = k * (1.0 + (a - 1.0) * ka_ref[...])
        kkn_ref[0, rows, :] = kkn
        b_ref[0, rows, :] = kkn * a
    _row_loop(tt, RB, out_rows)


def _block_diag(yb, bdmask):
    y2 = jnp.concatenate([yb, yb], axis=0)
    return jnp.where(bdmask, y2, jnp.zeros_like(y2))


def _pairs_chunk(ins, m_bd, consts):
    ltri, eye2, trimask, bdmask, bdmask32, lane0, ones_c = consts
    pairs = range(len(ins))
    ldc, rc, kc, vc, kknc, bc = ([x[i] for x in ins] for i in range(6))
    ld_hi = [ldc[p].astype(BF16) for p in pairs]
    ld_lo = [(ldc[p] - ld_hi[p].astype(F32)).astype(BF16) for p in pairs]
    cum = [_dot(ltri, ld_hi[p]) + _dot(ltri, ld_lo[p]) for p in pairs]
    wl_col = [jnp.exp(_dot_tn(ld_hi[p], ones_c) + _dot_tn(ld_lo[p], ones_c)) for p in pairs]
    atb, rtb, ys, bw, kw = [], [], [], [], []
    for p in pairs:
        w_inv = jnp.exp(-cum[p])
        bt = (bc[p] * w_inv).astype(BF16)
        kt = (kc[p] * w_inv).astype(BF16)
        zero = jnp.zeros_like(bt)
        ys.append(jnp.concatenate([jnp.where(lane0, bt, zero), jnp.where(lane0, zero, bt),
                                   jnp.where(lane0, kt, zero), jnp.where(lane0, zero, kt)], axis=0))
        atb.append((-(kknc[p] * jnp.exp(cum[p] - ldc[p]))).astype(BF16))
        rtb.append((rc[p] * jnp.exp(cum[p])).astype(BF16))
        w_rest = jnp.exp(cum[p][CHUNK - 1:CHUNK, :] - cum[p])
        bw.append((bc[p] * w_rest).astype(BF16))
        kw.append((kc[p] * w_rest).astype(BF16))
    sc = [jnp.where(trimask, _dot_nt(jnp.concatenate([atb[p], rtb[p]], axis=0), ys[p]), 0.0) for p in pairs]
    a_ab = [sc[p][0:CHUNK, 0:PAIR] for p in pairs]
    a_akb = [sc[p][0:CHUNK, PAIR:2 * PAIR].astype(BF16) for p in pairs]
    r_bb = [sc[p][CHUNK:2 * CHUNK, 0:PAIR].astype(BF16) for p in pairs]
    r_kb = [sc[p][CHUNK:2 * CHUNK, PAIR:2 * PAIR].astype(BF16) for p in pairs]

    n = [_dot(a_ab[p].astype(BF16), _block_diag(a_ab[p].astype(BF16), bdmask)) for p in pairs]
    t = [eye2 + a_ab[p] for p in pairs]
    for _ in range(4):
        res = [_dot(jnp.concatenate([n[p], t[p]], axis=0).astype(BF16), _block_diag(n[p].astype(BF16), bdmask))
               for p in pairs]
        n = [res[p][0:CHUNK] for p in pairs]
        t = [t[p] + res[p][CHUNK:2 * CHUNK] for p in pairs]
    t = [t[p] + _dot(t[p].astype(BF16), _block_diag(n[p].astype(BF16), bdmask)) for p in pairs]

    mb = [m_bd[p].astype(BF16) for p in pairs]
    vb = [vc[p].astype(BF16) for p in pairs]
    v_bd = [_block_diag(vb[p], bdmask) for p in pairs]
    q = [_dot(jnp.concatenate([atb[p], a_akb[p]], axis=1), jnp.concatenate([mb[p], v_bd[p]], axis=0)) for p in pairs]
    ub = [_dot(t[p].astype(BF16), _block_diag(q[p].astype(BF16), bdmask)).astype(BF16) for p in pairs]
    y = [_dot(jnp.concatenate([rtb[p], r_bb[p], r_kb[p]], axis=1),
              jnp.concatenate([mb[p], _block_diag(ub[p], bdmask), v_bd[p]], axis=0)) for p in pairs]
    upd = [_dot_tn(jnp.concatenate([bw[p], kw[p]], axis=0), jnp.concatenate([ub[p], vb[p]], axis=0)) for p in pairs]
    m_new = [wl_col[p] * m_bd[p] + jnp.where(bdmask32, upd[p], 0.0) for p in pairs]
    return y, m_new


def _mix_core_body(r_ref, ld_ref, k2_ref, v_ref, kkn_ref, b_ref, g_ref, ycg_ref, gb_ref, h_ref,
                   ltri_ref, eye2_ref, trimask_ref, bdmask_ref, bdmask32_ref, lane0_ref, gones_ref, gmean_ref,
                   rk_ref, lnxg_ref, lnxb_ref, wro_ref, wo_ref, ln1g_ref, ln1b_ref,
                   h1_ref, m_s, o_s, ob_s, t32_s, *, tt, alpha):
    ti = pl.program_id(1)

    @pl.when(ti == 0)
    def _():
        m_s[...] = jnp.zeros(m_s.shape, F32)

    consts = (ltri_ref[...], eye2_ref[...], trimask_ref[...] > 0.5, bdmask_ref[...] > 0.5,
              bdmask32_ref[...] > 0.5, lane0_ref[...] > 0.5, jnp.ones((CHUNK, PAIR), BF16))

    def chunk_body(c, carry):
        rows = pl.ds(pl.multiple_of(c * CHUNK, CHUNK), CHUNK)
        cols = [slice(p * PAIR, (p + 1) * PAIR) for p in range(N_PAIR)]
        ins = [(ld_ref[0, rows, cs], r_ref[0, rows, cs], k2_ref[0, rows, cs], v_ref[0, rows, cs],
                kkn_ref[0, rows, cs], b_ref[0, rows, cs]) for cs in cols]
        y, m_new = _pairs_chunk(ins, [m_s[p] for p in range(N_PAIR)], consts)
        for p in range(N_PAIR):
            m_s[p] = m_new[p]
            o_s[rows, cols[p]] = y[p]
        return carry
    lax.fori_loop(0, tt // CHUNK, chunk_body, 0)

    def ob_rows(base):
        rows = pl.ds(base, RB)
        ob_s[rows, :] = o_s[rows, :].astype(BF16)
    _row_loop(tt, RB, ob_rows)
    t32_s[...] = _seg_sum(ob_s[...], gmean_ref)

    def xc_rows(base):
        rows = pl.ds(base, RB)
        xc = o_s[rows, :] - t32_s[rows, :]
        o_s[rows, :] = xc
        ob_s[rows, :] = (xc * xc).astype(BF16)
    _row_loop(tt, RB, xc_rows)
    t32_s[...] = _seg_sum(ob_s[...], gmean_ref)

    def on_rows(base):
        rows = pl.ds(base, RB)
        on = o_s[rows, :] * lax.rsqrt(t32_s[rows, :] + GN_EPS) * lnxg_ref[...] + lnxb_ref[...]
        o_s[rows, :] = on
        ob_s[rows, :] = (r_ref[0, rows, :] * k2_ref[0, rows, :] * rk_ref[...]).astype(BF16)
    _row_loop(tt, RB, on_rows)
    t32_s[...] = _seg_sum(ob_s[...], gones_ref)

    def o2_rows(base):
        rows = pl.ds(base, RB)
        o2 = (o_s[rows, :] + t32_s[rows, :] * v_ref[0, rows, :]) * g_ref[0, rows, :]
        ob_s[rows, :] = o2.astype(BF16)
    _row_loop(tt, RB, o2_rows)

    y = ycg_ref[0] + gb_ref[0] * _dot(ob_s[...], wro_ref[...])
    t32_s[...] = _dot(y.astype(BF16), wo_ref[...])

    def ln1_rows(base):
        rows = pl.ds(base, RB)
        h1_ref[0, rows, :] = _layer_norm(alpha * h_ref[0, rows, :] + t32_s[rows, :],
                                         ln1g_ref[...], ln1b_ref[...], LN_EPS)
    _row_loop(tt, RB, ln1_rows, unroll=4)


def _ffn_body(h1_ref, wup_ref, dw_ref, wdn_ref, ln2g_ref, ln2b_ref, out_ref,
              hb_s, u_s, act_s, y_s, *, tt, alpha):
    ti = pl.program_id(1)
    first_real = FRONT - N_META

    @pl.when(ti == 0)
    def _():
        u_s[0:8, :] = jnp.zeros((8, 2 * D_FF), F32)

    def hb_rows(base):
        rows = pl.ds(base, RB)
        grow = ti * tt + base + lax.broadcasted_iota(jnp.int32, (RB, 1), 0)
        hb_s[rows, :] = jnp.where(grow >= first_real, h1_ref[0, rows, :], 0.0).astype(BF16)
    _row_loop(tt, RB, hb_rows)

    u_s[8:8 + tt, :] = _dot(hb_s[...], wup_ref[...])

    def conv_rows(base):
        rows = pl.ds(base, RB)

        def conv(cols):
            win = u_s[pl.ds(base, RB + 8), cols]
            acc = dw_ref[2:3, cols] * win[8:8 + RB]
            for d in (1, 2):
                acc = acc + dw_ref[2 - d:3 - d, cols] * pltpu.roll(win, d, axis=0)[8:8 + RB]
            return acc

        for c in range(D_FF // 256):
            lin = conv(slice(c * 256, (c + 1) * 256))
            gate = conv(slice(D_FF + c * 256, D_FF + (c + 1) * 256))
            act_s[rows, c * 256:(c + 1) * 256] = (gate * _sigmoid(gate) * lin).astype(BF16)
    _row_loop(tt, RB, conv_rows)

    u_s[0:8, :] = u_s[tt:tt + 8, :]
    y_s[...] = _dot(act_s[...], wdn_ref[...])

    def ln2_rows(base):
        rows = pl.ds(base, RB)
        out_ref[0, rows, :] = _layer_norm(alpha * h1_ref[0, rows, :] + y_s[rows, :],
                                          ln2g_ref[...], ln2b_ref[...], LN_EPS)
    _row_loop(tt, RB, ln2_rows, unroll=4)


def _const(shape):
    nd = len(shape)
    return pl.BlockSpec(shape, lambda b, t: (0,) * nd, pipeline_mode=pl.Buffered(1))


def _row2d(v):
    return v.reshape(1, -1).astype(F32)


def _mask_constants():
    t = np.arange(2 * CHUNK)[:, None] % CHUNK
    s = np.arange(2 * PAIR)[None, :] % RWKV_HEAD
    strict = (t > s)
    incl = (t >= s)
    trimask = np.where(np.arange(2 * CHUNK)[:, None] < CHUNK, strict, incl).astype(np.float32)
    hh = np.arange(PAIR) // RWKV_HEAD
    bdmask = (hh[:, None] == hh[None, :]).astype(np.float32)
    ltri = np.tril(np.ones((CHUNK, CHUNK), np.float32))
    eye2 = np.concatenate([np.eye(CHUNK, dtype=np.float32)] * 2, axis=1)
    lane0 = np.broadcast_to((np.arange(PAIR) < RWKV_HEAD)[None, :], (CHUNK, PAIR)).astype(np.float32)
    h4 = np.arange(256) // RWKV_HEAD
    gones = (h4[:, None] == h4[None, :]).astype(np.float32)
    return dict(ltri=jnp.asarray(ltri, BF16), eye2=jnp.asarray(eye2), trimask=jnp.asarray(trimask),
                bdmask=jnp.asarray(bdmask, BF16), bdmask32=jnp.asarray(bdmask), lane0=jnp.asarray(lane0, BF16),
                gones=jnp.asarray(gones, BF16), gmean=jnp.asarray(gones / RWKV_HEAD, BF16))


def _layer(x, prm, consts, bsz, n_tiles_real, alpha):
    d = D_MODEL
    tp = FRONT + n_tiles_real * TT
    c_end = 2 * d
    i_r = c_end
    i_x = i_r + 3 * d
    r_end = i_x + DECAY_LORA + ICLR_LORA + GATE_LORA
    w_in = prm["w_in"]
    wc = w_in[:, :c_end].astype(BF16)
    wrkv = w_in[:, i_r:i_x].astype(BF16)
    wx = jnp.pad(w_in[:, i_x:r_end], ((0, 0), (0, X_COLS - (r_end - i_x)))).astype(BF16)
    wgt = w_in[:, r_end:].astype(BF16)
    mu = prm["rwkv_mu"]
    mu_rkv = _row2d(mu[:3 * d])
    mu_x = _row2d(jnp.pad(mu[3 * d:], (0, X_COLS - (r_end - i_x))))
    wdu = jnp.pad(prm["w_decay_up"], ((0, 128 - DECAY_LORA), (0, 0))).astype(BF16)
    wau = jnp.pad(prm["a_up"], ((DECAY_LORA, 0), (0, 0))).astype(BF16)
    wgu = jnp.pad(prm["g_up"], ((0, 256 - GATE_LORA), (0, 0))).astype(BF16)

    tt = TT_PREP
    n_front = FRONT // tt
    n_t = tp // tt
    tile = pl.BlockSpec((1, tt, d), lambda b, t: (b, t, 0))
    in_specs = [
        pl.BlockSpec((1, tt, d), lambda b, t: (b, jnp.maximum(t - n_front, 0), 0)),
        pl.BlockSpec((tt, d), lambda b, t: (jnp.minimum(t, n_front - 1), 0)),
        _const((1, d)), _const((1, d)),
        _const((d, 2 * d)), _const((d, 3 * d)), _const((d, X_COLS)), _const((d, 2 * d)), _const((1, 2 * d)),
        _const((CONV_KERNEL, d)), _const((1, d)), _const((1, d)), _const((1, d)), _const((d, d)),
        _const((1, 3 * d)), _const((1, X_COLS)),
        _const((1, d)), _const((128, d)), _const((1, d)), _const((128, d)), _const((256, d)),
        _const((1, d)), _const((1, d)), _const((256, 256)),
    ]
    act = jax.ShapeDtypeStruct((bsz, tp, d), F32)
    scratch = [
        pltpu.VMEM((tt, d), BF16),
        pltpu.VMEM((tt, 3 * d), F32),
        pltpu.VMEM((tt + HALO, d), F32),
        pltpu.VMEM((tt, d), BF16),
        pltpu.VMEM((tt + 8, 3 * d + X_COLS), F32),
        pltpu.VMEM((tt, 128), BF16), pltpu.VMEM((tt, 128), BF16), pltpu.VMEM((tt, 256), BF16),
        pltpu.VMEM((tt, d), F32), pltpu.VMEM((tt, d), F32), pltpu.VMEM((tt, d), BF16),
    ]
    outs = pl.pallas_call(
        functools.partial(_mix_prep_body, tt=tt),
        grid=(bsz, n_t), in_specs=in_specs, out_specs=[tile] * 10, out_shape=[act] * 10,
        scratch_shapes=scratch, name="mix_prep",
        compiler_params=pltpu.CompilerParams(dimension_semantics=("arbitrary", "arbitrary"),
                                             vmem_limit_bytes=VMEM_LIMIT),
    )(x, prm["meta_pad"], prm["ln_in_g"], prm["ln_in_b"], wc, wrkv, wx, wgt, _row2d(prm["b_gate"]),
      prm["conv_dw"].astype(F32), _row2d(prm["conv_dw_b"]), _row2d(prm["conv_ln_g"]), _row2d(prm["conv_ln_b"]),
      prm["w_conv_out"].astype(BF16), mu_rkv, mu_x,
      _row2d(prm["w0"]), wdu, _row2d(prm["a0"]), wau, wgu, _row2d(prm["k_k"]), _row2d(prm["k_a"]),
      consts["gones"])
    h, r, ld, k2, v, kkn, b, g, ycg, gb = outs

    tt = TT
    n_t = tp // tt
    tile = pl.BlockSpec((1, tt, d), lambda b, t: (b, t, 0))
    in_specs = [tile] * 10 + [
        _const((CHUNK, CHUNK)), _const((CHUNK, PAIR)), _const((2 * CHUNK, 2 * PAIR)), _const((PAIR, PAIR)),
        _const((PAIR, PAIR)),
        _const((CHUNK, PAIR)), _const((256, 256)), _const((256, 256)),
        _const((1, d)), _const((1, d)), _const((1, d)), _const((d, d)), _const((d, d)), _const((1, d)), _const((1, d)),
    ]
    scratch = [
        pltpu.VMEM((N_PAIR, PAIR, PAIR), F32),
        pltpu.VMEM((tt, d), F32), pltpu.VMEM((tt, d), BF16), pltpu.VMEM((tt, d), F32),
    ]
    h1 = pl.pallas_call(
        functools.partial(_mix_core_body, tt=tt, alpha=alpha),
        grid=(bsz, n_t), in_specs=in_specs, out_specs=tile, out_shape=act,
        scratch_shapes=scratch, name="mix_core",
        compiler_params=pltpu.CompilerParams(dimension_semantics=("arbitrary", "arbitrary"),
                                             vmem_limit_bytes=VMEM_LIMIT),
    )(r, ld, k2, v, kkn, b, g, ycg, gb, h,
      consts["ltri"], consts["eye2"], consts["trimask"], consts["bdmask"], consts["bdmask32"], consts["lane0"],
      consts["gones"], consts["gmean"],
      _row2d(prm["r_k"]), _row2d(prm["lnx_g"]), _row2d(prm["lnx_b"]),
      prm["w_rwkv_out"].astype(BF16), prm["w_o"].astype(BF16), _row2d(prm["ln1_g"]), _row2d(prm["ln1_b"]))

    n_front = FRONT // tt
    in_specs = [tile, _const((d, 2 * D_FF)), _const((FFN_KERNEL, 2 * D_FF)), _const((D_FF, d)),
                _const((1, d)), _const((1, d))]
    scratch = [pltpu.VMEM((tt, d), BF16), pltpu.VMEM((tt + 8, 2 * D_FF), F32),
               pltpu.VMEM((tt, D_FF), BF16), pltpu.VMEM((tt, d), F32)]
    out = pl.pallas_call(
        functools.partial(_ffn_body, tt=tt, alpha=alpha),
        grid=(bsz, n_t), in_specs=in_specs,
        out_specs=pl.BlockSpec((1, tt, d), lambda b, t: (b, jnp.maximum(t - n_front, 0), 0)),
        out_shape=jax.ShapeDtypeStruct((bsz, n_tiles_real * tt, d), F32),
        scratch_shapes=scratch, name="ffn",
        compiler_params=pltpu.CompilerParams(dimension_semantics=("arbitrary", "arbitrary"),
                                             vmem_limit_bytes=VMEM_LIMIT),
    )(h1, prm["ffn_up"].astype(BF16), prm["ffn_dw"].astype(F32), prm["ffn_down"].astype(BF16),
      _row2d(prm["ln2_g"]), _row2d(prm["ln2_b"]))
    return out


def kernel(x, meta_tokens, ln_in_g, ln_in_b, w_in, b_gate, conv_dw, conv_dw_b, conv_ln_g, conv_ln_b, w_conv_out, rwkv_mu, w0, w_decay_up, a0, a_up, g_up, k_k, k_a, r_k, lnx_g, lnx_b, w_rwkv_out, w_o, ln1_g, ln1_b, ffn_up, ffn_dw, ffn_down, ln2_g, ln2_b):
    bsz, seq, d = x.shape
    depth = w_in.shape[0]
    assert d == D_MODEL and seq % TT == 0 and depth == 1, (x.shape, w_in.shape)
    alpha = (2.0 * depth) ** 0.25
    consts = _mask_constants()
    meta_pad = jnp.concatenate([jnp.zeros((FRONT - N_META, d), F32), meta_tokens.astype(F32)], axis=0)
    prm = dict(meta_pad=meta_pad, ln_in_g=_row2d(ln_in_g), ln_in_b=_row2d(ln_in_b),
               w_in=w_in[0], b_gate=b_gate[0], conv_dw=conv_dw[0], conv_dw_b=conv_dw_b[0],
               conv_ln_g=conv_ln_g[0], conv_ln_b=conv_ln_b[0], w_conv_out=w_conv_out[0], rwkv_mu=rwkv_mu[0],
               w0=w0[0], w_decay_up=w_decay_up[0], a0=a0[0], a_up=a_up[0], g_up=g_up[0], k_k=k_k[0], k_a=k_a[0],
               r_k=r_k[0], lnx_g=lnx_g[0], lnx_b=lnx_b[0], w_rwkv_out=w_rwkv_out[0], w_o=w_o[0],
               ln1_g=ln1_g[0], ln1_b=ln1_b[0], ffn_up=ffn_up[0], ffn_dw=ffn_dw[0], ffn_down=ffn_down[0],
               ln2_g=ln2_g[0], ln2_b=ln2_b[0])
    return _layer(x.astype(F32), prm, consts, bsz, seq // TT, alpha)
```

```python
import functools
import math

import jax
import jax.numpy as jnp
import numpy as np
from jax import lax
from jax.experimental import pallas as pl
from jax.experimental.pallas import tpu as pltpu

D_MODEL = 1024
N_META = 16
CONV_KERNEL = 31
RWKV_HEAD = 64
RWKV_HEADS = D_MODEL // RWKV_HEAD
DECAY_LORA = 64
ICLR_LORA = 64
GATE_LORA = 160
D_FF = 2816
FFN_KERNEL = 3
LN_EPS = 1e-5
GN_EPS = 64e-5

FRONT = 256
TT_PREP = 128
TT = 256
CHUNK = 64
PAIR = 2 * RWKV_HEAD
N_PAIR = D_MODEL // PAIR
HALO = 32
RB = 16
FB = 256
X_COLS = 384
VMEM_LIMIT = 56 * 1024 * 1024

F32 = jnp.float32
BF16 = jnp.bfloat16


def _dot(a, b, **kw):
    return jnp.dot(a, b, preferred_element_type=F32, **kw)


def _dot_nt(a, b):
    return lax.dot_general(a, b, (((1,), (1,)), ((), ())), preferred_element_type=F32)


def _dot_tn(a, b, **kw):
    return lax.dot_general(a, b, (((0,), (0,)), ((), ())), preferred_element_type=F32, **kw)


def _sigmoid(x):
    return 1.0 / (1.0 + jnp.exp(-x))


def _layer_norm(x, g, b, eps):
    mu = jnp.mean(x, axis=-1, keepdims=True)
    xc = x - mu
    var = jnp.mean(xc * xc, axis=-1, keepdims=True)
    return xc * lax.rsqrt(var + eps) * g + b


def _row_loop(n_rows, rb, fn, unroll=1):
    def body(i, c):
        fn(pl.multiple_of(i * rb, rb))
        return c
    lax.fori_loop(0, n_rows // rb, body, 0, unroll=unroll)


def _seg_sum(xb, g_ref):
    parts = [_dot(xb[:, c * 256:(c + 1) * 256], g_ref[...]) for c in range(D_MODEL // 256)]
    return jnp.concatenate(parts, axis=1)


def _mix_prep_body(x_ref, meta_ref, lng_ref, lnb_ref, wc_ref, wrkv_ref, wx_ref, wgt_ref, bgate_ref,
                   convw_ref, convb_ref, clng_ref, clnb_ref, wco_ref, mu_rkv_ref, mu_x_ref,
                   w0_ref, wdu_ref, a0_ref, wau_ref, wgu_ref, kk_ref, ka_ref, gones_ref,
                   h_ref, r_ref, ld_ref, k2_ref, v_ref, kkn_ref, b_ref, g_ref, ycg_ref, gb_ref,
                   hb_s, pc_s, cbuf_s, conv_s, cact_s, pbuf_s, rkv_s, pg_s, t1_s, x1_s, s2_s,
                   z1_s, z2_s, kk2_s, ss_s, yc_s, *, tt):
    ti = pl.program_id(1)
    n_skip = (FRONT - N_META) // tt
    outs = (h_ref, r_ref, ld_ref, k2_ref, v_ref, kkn_ref, b_ref, g_ref, ycg_ref, gb_ref)

    @pl.when(ti == 0)
    def _():
        cbuf_s[0:HALO, :] = jnp.zeros((HALO, D_MODEL), F32)
        pbuf_s[0:8, :] = jnp.zeros((8, pbuf_s.shape[1]), F32)

    @pl.when(ti < n_skip)
    def _():
        for o in outs:
            o[...] = jnp.zeros(o.shape, o.dtype)

    @pl.when(ti >= n_skip)
    def _():
        _mix_prep_tile(x_ref, meta_ref, lng_ref, lnb_ref, wc_ref, wrkv_ref, wx_ref, wgt_ref, bgate_ref,
                       convw_ref, convb_ref, clng_ref, clnb_ref, wco_ref, mu_rkv_ref, mu_x_ref,
                       w0_ref, wdu_ref, a0_ref, wau_ref, wgu_ref, kk_ref, ka_ref, gones_ref, *outs,
                       hb_s, pc_s, cbuf_s, conv_s, cact_s, pbuf_s, rkv_s, pg_s, t1_s, x1_s, s2_s,
                       z1_s, z2_s, kk2_s, ss_s, yc_s, tt=tt)


def _mix_prep_tile(x_ref, meta_ref, lng_ref, lnb_ref, wc_ref, wrkv_ref, wx_ref, wgt_ref, bgate_ref,
                   convw_ref, convb_ref, clng_ref, clnb_ref, wco_ref, mu_rkv_ref, mu_x_ref,
                   w0_ref, wdu_ref, a0_ref, wau_ref, wgu_ref, kk_ref, ka_ref, gones_ref,
                   h_ref, r_ref, ld_ref, k2_ref, v_ref, kkn_ref, b_ref, g_ref, ycg_ref, gb_ref,
                   hb_s, pc_s, cbuf_s, conv_s, cact_s, pbuf_s, rkv_s, pg_s, t1_s, x1_s, s2_s,
                   z1_s, z2_s, kk2_s, ss_s, yc_s, *, tt):
    ti = pl.program_id(1)
    n_front = FRONT // tt
    first_real = FRONT - N_META
    n_rkv = 3 * D_MODEL

    def ln_rows(base):
        rows = pl.ds(base, RB)
        xin = jnp.where(ti < n_front, meta_ref[rows, :], x_ref[0, rows, :])
        h = _layer_norm(xin, lng_ref[...], lnb_ref[...], LN_EPS)
        grow = ti * tt + base + lax.broadcasted_iota(jnp.int32, (RB, 1), 0)
        h = jnp.where(grow >= first_real, h, 0.0)
        h_ref[0, rows, :] = h
        hb_s[rows, :] = h.astype(BF16)
    _row_loop(tt, RB, ln_rows, unroll=4)

    hb = hb_s[...]
    pc_s[...] = _dot(hb, wc_ref[...])
    pbuf_s[8:8 + tt, 0:n_rkv] = _dot(hb, wrkv_ref[...])
    pbuf_s[8:8 + tt, n_rkv:n_rkv + X_COLS] = _dot(hb, wx_ref[...])
    pg_s[...] = _dot(hb, wgt_ref[...]) + bgate_ref[...]

    for base in range(0, tt, RB):
        rows = slice(base, base + RB)
        c = pc_s[rows, 0:D_MODEL] * _sigmoid(pc_s[rows, D_MODEL:2 * D_MODEL])
        cbuf_s[base + HALO:base + HALO + RB, :] = c

    cr = 64
    for base in range(0, tt, cr):
        for cb in range(D_MODEL // 128):
            cs = slice(cb * 128, (cb + 1) * 128)
            win = cbuf_s[base:base + cr + HALO, cs]
            acc = jnp.zeros((cr, 128), F32) + convb_ref[:, cs]
            for r in range(8):
                sh = win if r == 0 else pltpu.roll(win, r, axis=0)
                for q in range(4):
                    d = 8 * q + r
                    if d > CONV_KERNEL - 1:
                        continue
                    j = CONV_KERNEL - 1 - d
                    acc = acc + convw_ref[j:j + 1, cs] * sh[HALO - 8 * q:HALO - 8 * q + cr]
            conv_s[base:base + cr, cs] = acc
    cbuf_s[0:HALO, :] = cbuf_s[tt:tt + HALO, :]

    def lerp(base, cols, mu):
        win = pbuf_s[base:base + RB + 8, cols]
        cur = win[8:8 + RB]
        prev = pltpu.roll(win, 1, axis=0)[8:8 + RB]
        return cur + (prev - cur) * mu

    for base in range(0, tt, RB):
        rows = slice(base, base + RB)
        for c in range(n_rkv // 256):
            cols = slice(c * 256, (c + 1) * 256)
            rkv_s[rows, cols] = lerp(base, cols, mu_rkv_ref[:, cols])
        x1 = lerp(base, slice(n_rkv, n_rkv + 128), mu_x_ref[:, 0:128])
        x2 = lerp(base, slice(n_rkv + 128, n_rkv + X_COLS), mu_x_ref[:, 128:X_COLS])
        t1_s[rows, :] = jnp.tanh(x1).astype(BF16)
        x1_s[rows, :] = x1.astype(BF16)
        s2_s[rows, :] = _sigmoid(x2).astype(BF16)
        kk = rkv_s[rows, D_MODEL:2 * D_MODEL] * kk_ref[...]
        kk2_s[rows, :] = (kk * kk).astype(BF16)
    pbuf_s[0:8, :] = pbuf_s[tt:tt + 8, :]

    z1_s[...] = _dot(t1_s[...], wdu_ref[...])
    z2_s[...] = _dot(x1_s[...], wau_ref[...])
    g_ref[0] = _dot(s2_s[...], wgu_ref[...]).astype(g_ref.dtype)
    ss_s[...] = _seg_sum(kk2_s[...], gones_ref)

    for base in range(0, tt, RB):
        rows = slice(base, base + RB)
        y = _layer_norm(conv_s[rows, :], clng_ref[...], clnb_ref[...], LN_EPS)
        cact_s[rows, :] = (y * _sigmoid(y)).astype(BF16)
    yc_s[...] = _dot(cact_s[...], wco_ref[...])

    for base in range(0, tt, RB):
        rows = slice(base, base + RB)
        r = rkv_s[rows, 0:D_MODEL]
        k = rkv_s[rows, D_MODEL:2 * D_MODEL]
        v = rkv_s[rows, 2 * D_MODEL:3 * D_MODEL]
        ld = -math.exp(-0.5) * _sigmoid(w0_ref[...] + z1_s[rows, :])
        a = _sigmoid(a0_ref[...] + z2_s[rows, :])
        kkn = (k * kk_ref[...]) * jnp.minimum(lax.rsqrt(ss_s[rows, :]), 1e12)
        r_ref[0, rows, :] = r
        v_ref[0, rows, :] = v
        ld_ref[0, rows, :] = ld
        k2_ref[0, rows, :] = k * (1.0 + (a - 1.0) * ka_ref[...])
        kkn_ref[0, rows, :] = kkn
        b_ref[0, rows, :] = kkn * a

    for base in range(0, tt, RB):
        rows = slice(base, base + RB)
        ycg_ref[0, rows, :] = (_sigmoid(pg_s[rows, 0:D_MODEL]) * yc_s[rows, :]).astype(ycg_ref.dtype)
        gb_ref[0, rows, :] = _sigmoid(pg_s[rows, D_MODEL:2 * D_MODEL]).astype(gb_ref.dtype)


def _block_diag(yb, bdmask):
    y2 = jnp.concatenate([yb, yb], axis=0)
    return jnp.where(bdmask, y2, jnp.zeros_like(y2))


def _pairs_chunk(ins, m_bd, consts):
    ltri, eye2, trimask, bdmask, bdmask32, lane0, ones_c = consts
    pairs = range(len(ins))
    ldc, rc, kc, vc, kknc, bc = ([x[i] for x in ins] for i in range(6))
    ld_hi = [ldc[p].astype(BF16) for p in pairs]
    ld_lo = [(ldc[p] - ld_hi[p].astype(F32)).astype(BF16) for p in pairs]
    cum = [_dot(ltri, ld_hi[p]) + _dot(ltri, ld_lo[p]) for p in pairs]
    wl_col = [jnp.exp(_dot_tn(ld_hi[p], ones_c) + _dot_tn(ld_lo[p], ones_c)) for p in pairs]
    atb, rtb, ys, bw, kw = [], [], [], [], []
    for p in pairs:
        w_inv = jnp.exp(-cum[p])
        bt = (bc[p] * w_inv).astype(BF16)
        kt = (kc[p] * w_inv).astype(BF16)
        zero = jnp.zeros_like(bt)
        ys.append(jnp.concatenate([jnp.where(lane0, bt, zero), jnp.where(lane0, zero, bt),
                                   jnp.where(lane0, kt, zero), jnp.where(lane0, zero, kt)], axis=0))
        atb.append((-(kknc[p] * jnp.exp(cum[p] - ldc[p]))).astype(BF16))
        rtb.append((rc[p] * jnp.exp(cum[p])).astype(BF16))
        w_rest = jnp.exp(cum[p][CHUNK - 1:CHUNK, :] - cum[p])
        bw.append((bc[p] * w_rest).astype(BF16))
        kw.append((kc[p] * w_rest).astype(BF16))
    sc = [jnp.where(trimask, _dot_nt(jnp.concatenate([atb[p], rtb[p]], axis=0), ys[p]), 0.0) for p in pairs]
    a_ab = [sc[p][0:CHUNK, 0:PAIR] for p in pairs]
    a_akb = [sc[p][0:CHUNK, PAIR:2 * PAIR].astype(BF16) for p in pairs]
    r_bb = [sc[p][CHUNK:2 * CHUNK, 0:PAIR].astype(BF16) for p in pairs]
    r_kb = [sc[p][CHUNK:2 * CHUNK, PAIR:2 * PAIR].astype(BF16) for p in pairs]

    n = [_dot(a_ab[p].astype(BF16), _block_diag(a_ab[p].astype(BF16), bdmask)) for p in pairs]
    t = [eye2 + a_ab[p] for p in pairs]
    for _ in range(4):
        res = [_dot(jnp.concatenate([n[p], t[p]], axis=0).astype(BF16), _block_diag(n[p].astype(BF16), bdmask))
               for p in pairs]
        n = [res[p][0:CHUNK] for p in pairs]
        t = [t[p] + res[p][CHUNK:2 * CHUNK] for p in pairs]
    t = [t[p] + _dot(t[p].astype(BF16), _block_diag(n[p].astype(BF16), bdmask)) for p in pairs]

    mb = [m_bd[p].astype(BF16) for p in pairs]
    vb = [vc[p].astype(BF16) for p in pairs]
    v_bd = [_block_diag(vb[p], bdmask) for p in pairs]
    q = [_dot(jnp.concatenate([atb[p], a_akb[p]], axis=1), jnp.concatenate([mb[p], v_bd[p]], axis=0)) for p in pairs]
    ub = [_dot(t[p].astype(BF16), _block_diag(q[p].astype(BF16), bdmask)).astype(BF16) for p in pairs]
    y = [_dot(jnp.concatenate([rtb[p], r_bb[p], r_kb[p]], axis=1),
              jnp.concatenate([mb[p], _block_diag(ub[p], bdmask), v_bd[p]], axis=0)) for p in pairs]
    upd = [_dot_tn(jnp.concatenate([bw[p], kw[p]], axis=0), jnp.concatenate([ub[p], vb[p]], axis=0)) for p in pairs]
    m_new = [wl_col[p] * m_bd[p] + jnp.where(bdmask32, upd[p], 0.0) for p in pairs]
    return y, m_new


def _mix_core_body(r_ref, ld_ref, k2_ref, v_ref, kkn_ref, b_ref, g_ref, ycg_ref, gb_ref, h_ref,
                   ltri_ref, eye2_ref, trimask_ref, bdmask_ref, bdmask32_ref, lane0_ref, gones_ref, gmean_ref,
                   rk_ref, lnxg_ref, lnxb_ref, wro_ref, wo_ref, ln1g_ref, ln1b_ref,
                   h1_ref, m_s, o_s, ob_s, t32_s, rk_s, bon_s, *, tt, alpha):
    ti = pl.program_id(1)
    n_skip = (FRONT - N_META) // CHUNK

    @pl.when(ti == 0)
    def _():
        m_s[...] = jnp.zeros(m_s.shape, F32)
        o_s[0:n_skip * CHUNK, :] = jnp.zeros((n_skip * CHUNK, D_MODEL), F32)

    consts = (ltri_ref[...], eye2_ref[...], trimask_ref[...] > 0.5, bdmask_ref[...] > 0.5,
              bdmask32_ref[...] > 0.5, lane0_ref[...] > 0.5, jnp.ones((CHUNK, PAIR), BF16))

    def chunk_body(c, carry):
        rows = pl.ds(pl.multiple_of(c * CHUNK, CHUNK), CHUNK)
        cols = [slice(p * PAIR, (p + 1) * PAIR) for p in range(N_PAIR)]
        ins = [(ld_ref[0, rows, cs], r_ref[0, rows, cs], k2_ref[0, rows, cs], v_ref[0, rows, cs],
                kkn_ref[0, rows, cs], b_ref[0, rows, cs]) for cs in cols]
        y, m_new = _pairs_chunk(ins, [m_s[p] for p in range(N_PAIR)], consts)
        for p in range(N_PAIR):
            m_s[p] = m_new[p]
            o_s[rows, cols[p]] = y[p]
        return carry
    lax.fori_loop(jnp.where(ti == 0, n_skip, 0), tt // CHUNK, chunk_body, 0)

    blocks = [slice(base, base + RB) for base in range(0, tt, RB)]
    for rows in blocks:
        ob_s[rows, :] = o_s[rows, :].astype(BF16)
    t32_s[...] = _seg_sum(ob_s[...], gmean_ref)
    for rows in blocks:
        xc = o_s[rows, :] - t32_s[rows, :]
        o_s[rows, :] = xc
        ob_s[rows, :] = (xc * xc).astype(BF16)
        rk_s[rows, :] = (r_ref[0, rows, :] * k2_ref[0, rows, :] * rk_ref[...]).astype(BF16)
    t32_s[...] = _seg_sum(ob_s[...], gmean_ref)
    bon_s[...] = _seg_sum(rk_s[...], gones_ref)
    for rows in blocks:
        on = o_s[rows, :] * lax.rsqrt(t32_s[rows, :] + GN_EPS) * lnxg_ref[...] + lnxb_ref[...]
        o2 = (on + bon_s[rows, :] * v_ref[0, rows, :]) * g_ref[0, rows, :].astype(F32)
        ob_s[rows, :] = o2.astype(BF16)
    t32_s[...] = _dot(ob_s[...], wro_ref[...])
    for rows in blocks:
        y = ycg_ref[0, rows, :].astype(F32) + gb_ref[0, rows, :].astype(F32) * t32_s[rows, :]
        ob_s[rows, :] = y.astype(BF16)
    t32_s[...] = _dot(ob_s[...], wo_ref[...])
    for rows in blocks:
        h1_ref[0, rows, :] = _layer_norm(alpha * h_ref[0, rows, :] + t32_s[rows, :],
                                         ln1g_ref[...], ln1b_ref[...], LN_EPS)


def _ffn_body(h1_ref, wup_ref, dw_ref, wdn_ref, ln2g_ref, ln2b_ref, out_ref,
              hb_s, carry_s, y_s, *, tt, alpha):
    ti = pl.program_id(1)
    first_real = FRONT - N_META

    @pl.when(ti == 0)
    def _():
        carry_s[...] = jnp.zeros(carry_s.shape, F32)

    def hb_rows(base):
        rows = pl.ds(base, RB)
        grow = ti * tt + base + lax.broadcasted_iota(jnp.int32, (RB, 1), 0)
        hb_s[rows, :] = jnp.where(grow >= first_real, h1_ref[0, rows, :], 0.0).astype(BF16)
    _row_loop(tt, RB, hb_rows)

    n_blk = D_FF // FB

    def up(j):
        hb = hb_s[...]
        return [_dot(hb, wup_ref[:, half * D_FF + j * FB:half * D_FF + (j + 1) * FB]) for half in (0, 1)]

    def conv(cols, u):
        ext = jnp.concatenate([carry_s[:, cols], u], axis=0)
        carry_s[:, cols] = u[tt - 8:tt]
        return (dw_ref[2:3, cols] * u + dw_ref[1:2, cols] * pltpu.roll(ext, 1, axis=0)[8:8 + tt]
                + dw_ref[0:1, cols] * pltpu.roll(ext, 2, axis=0)[8:8 + tt])

    def act(j, u):
        lin = conv(slice(j * FB, (j + 1) * FB), u[0])
        gate = conv(slice(D_FF + j * FB, D_FF + (j + 1) * FB), u[1])
        return (gate * _sigmoid(gate) * lin).astype(BF16)

    u_next = up(0)
    for j in range(n_blk):
        u = u_next
        if j + 1 < n_blk:
            u_next = up(j + 1)
        d = _dot(act(j, u), wdn_ref[j * FB:(j + 1) * FB, :])
        if j == 0:
            y_s[...] = d
        else:
            y_s[...] += d

    def ln2_rows(base):
        rows = pl.ds(base, RB)
        out_ref[0, rows, :] = _layer_norm(alpha * h1_ref[0, rows, :] + y_s[rows, :],
                                          ln2g_ref[...], ln2b_ref[...], LN_EPS)
    _row_loop(tt, RB, ln2_rows, unroll=4)


def _const(shape):
    nd = len(shape)
    return pl.BlockSpec(shape, lambda b, t: (0,) * nd, pipeline_mode=pl.Buffered(1))


def _row2d(v):
    return v.reshape(1, -1).astype(F32)


def _mask_constants():
    t = np.arange(2 * CHUNK)[:, None] % CHUNK
    s = np.arange(2 * PAIR)[None, :] % RWKV_HEAD
    strict = (t > s)
    incl = (t >= s)
    trimask = np.where(np.arange(2 * CHUNK)[:, None] < CHUNK, strict, incl).astype(np.float32)
    hh = np.arange(PAIR) // RWKV_HEAD
    bdmask = (hh[:, None] == hh[None, :]).astype(np.float32)
    ltri = np.tril(np.ones((CHUNK, CHUNK), np.float32))
    eye2 = np.concatenate([np.eye(CHUNK, dtype=np.float32)] * 2, axis=1)
    lane0 = np.broadcast_to((np.arange(PAIR) < RWKV_HEAD)[None, :], (CHUNK, PAIR)).astype(np.float32)
    h4 = np.arange(256) // RWKV_HEAD
    gones = (h4[:, None] == h4[None, :]).astype(np.float32)
    return dict(ltri=jnp.asarray(ltri, BF16), eye2=jnp.asarray(eye2), trimask=jnp.asarray(trimask),
                bdmask=jnp.asarray(bdmask, BF16), bdmask32=jnp.asarray(bdmask), lane0=jnp.asarray(lane0, BF16),
                gones=jnp.asarray(gones, BF16), gmean=jnp.asarray(gones / RWKV_HEAD, BF16))


def _layer(x, prm, consts, bsz, n_tiles_real, alpha):
    d = D_MODEL
    tp = FRONT + n_tiles_real * TT
    c_end = 2 * d
    i_r = c_end
    i_x = i_r + 3 * d
    r_end = i_x + DECAY_LORA + ICLR_LORA + GATE_LORA
    w_in = prm["w_in"]
    wc = w_in[:, :c_end].astype(BF16)
    wrkv = w_in[:, i_r:i_x].astype(BF16)
    wx = jnp.pad(w_in[:, i_x:r_end], ((0, 0), (0, X_COLS - (r_end - i_x)))).astype(BF16)
    wgt = w_in[:, r_end:].astype(BF16)
    mu = prm["rwkv_mu"]
    mu_rkv = _row2d(mu[:3 * d])
    mu_x = _row2d(jnp.pad(mu[3 * d:], (0, X_COLS - (r_end - i_x))))
    wdu = jnp.pad(prm["w_decay_up"], ((0, 128 - DECAY_LORA), (0, 0))).astype(BF16)
    wau = jnp.pad(prm["a_up"], ((DECAY_LORA, 0), (0, 0))).astype(BF16)
    wgu = jnp.pad(prm["g_up"], ((0, 256 - GATE_LORA), (0, 0))).astype(BF16)

    tt = TT_PREP
    n_front = FRONT // tt
    n_t = tp // tt
    tile = pl.BlockSpec((1, tt, d), lambda b, t: (b, t, 0))
    in_specs = [
        pl.BlockSpec((1, tt, d), lambda b, t: (b, jnp.maximum(t - n_front, 0), 0)),
        pl.BlockSpec((tt, d), lambda b, t: (jnp.minimum(t, n_front - 1), 0)),
        _const((1, d)), _const((1, d)),
        _const((d, 2 * d)), _const((d, 3 * d)), _const((d, X_COLS)), _const((d, 2 * d)), _const((1, 2 * d)),
        _const((CONV_KERNEL, d)), _const((1, d)), _const((1, d)), _const((1, d)), _const((d, d)),
        _const((1, 3 * d)), _const((1, X_COLS)),
        _const((1, d)), _const((128, d)), _const((1, d)), _const((128, d)), _const((256, d)),
        _const((1, d)), _const((1, d)), _const((256, 256)),
    ]
    act = jax.ShapeDtypeStruct((bsz, tp, d), F32)
    act_bf = jax.ShapeDtypeStruct((bsz, tp, d), BF16)
    scratch = [
        pltpu.VMEM((tt, d), BF16),
        pltpu.VMEM((tt, 2 * d), F32),
        pltpu.VMEM((tt + HALO, d), F32),
        pltpu.VMEM((tt, d), F32),
        pltpu.VMEM((tt, d), BF16),
        pltpu.VMEM((tt + 8, 3 * d + X_COLS), F32),
        pltpu.VMEM((tt, 3 * d), F32),
        pltpu.VMEM((tt, 2 * d), F32),
        pltpu.VMEM((tt, 128), BF16), pltpu.VMEM((tt, 128), BF16), pltpu.VMEM((tt, 256), BF16),
        pltpu.VMEM((tt, d), F32), pltpu.VMEM((tt, d), F32), pltpu.VMEM((tt, d), BF16),
        pltpu.VMEM((tt, d), F32), pltpu.VMEM((tt, d), F32),
    ]
    outs = pl.pallas_call(
        functools.partial(_mix_prep_body, tt=tt),
        grid=(bsz, n_t), in_specs=in_specs, out_specs=[tile] * 10, out_shape=[act] * 7 + [act_bf] * 3,
        scratch_shapes=scratch, name="mix_prep",
        compiler_params=pltpu.CompilerParams(dimension_semantics=("arbitrary", "arbitrary"),
                                             vmem_limit_bytes=VMEM_LIMIT),
    )(x, prm["meta_pad"], prm["ln_in_g"], prm["ln_in_b"], wc, wrkv, wx, wgt, _row2d(prm["b_gate"]),
      prm["conv_dw"].astype(F32), _row2d(prm["conv_dw_b"]), _row2d(prm["conv_ln_g"]), _row2d(prm["conv_ln_b"]),
      prm["w_conv_out"].astype(BF16), mu_rkv, mu_x,
      _row2d(prm["w0"]), wdu, _row2d(prm["a0"]), wau, wgu, _row2d(prm["k_k"]), _row2d(prm["k_a"]),
      consts["gones"])
    h, r, ld, k2, v, kkn, b, g, ycg, gb = outs

    tt = TT
    n_t = tp // tt
    tile = pl.BlockSpec((1, tt, d), lambda b, t: (b, t, 0))
    in_specs = [tile] * 10 + [
        _const((CHUNK, CHUNK)), _const((CHUNK, PAIR)), _const((2 * CHUNK, 2 * PAIR)), _const((PAIR, PAIR)),
        _const((PAIR, PAIR)),
        _const((CHUNK, PAIR)), _const((256, 256)), _const((256, 256)),
        _const((1, d)), _const((1, d)), _const((1, d)), _const((d, d)), _const((d, d)), _const((1, d)), _const((1, d)),
    ]
    scratch = [
        pltpu.VMEM((N_PAIR, PAIR, PAIR), F32),
        pltpu.VMEM((tt, d), F32), pltpu.VMEM((tt, d), BF16), pltpu.VMEM((tt, d), F32),
        pltpu.VMEM((tt, d), BF16), pltpu.VMEM((tt, d), F32),
    ]
    h1 = pl.pallas_call(
        functools.partial(_mix_core_body, tt=tt, alpha=alpha),
        grid=(bsz, n_t), in_specs=in_specs, out_specs=tile, out_shape=act,
        scratch_shapes=scratch, name="mix_core",
        compiler_params=pltpu.CompilerParams(dimension_semantics=("arbitrary", "arbitrary"),
                                             vmem_limit_bytes=VMEM_LIMIT),
    )(r, ld, k2, v, kkn, b, g, ycg, gb, h,
      consts["ltri"], consts["eye2"], consts["trimask"], consts["bdmask"], consts["bdmask32"], consts["lane0"],
      consts["gones"], consts["gmean"],
      _row2d(prm["r_k"]), _row2d(prm["lnx_g"]), _row2d(prm["lnx_b"]),
      prm["w_rwkv_out"].astype(BF16), prm["w_o"].astype(BF16), _row2d(prm["ln1_g"]), _row2d(prm["ln1_b"]))

    n_front = FRONT // tt
    in_specs = [tile, _const((d, 2 * D_FF)), _const((FFN_KERNEL, 2 * D_FF)), _const((D_FF, d)),
                _const((1, d)), _const((1, d))]
    scratch = [pltpu.VMEM((tt, d), BF16), pltpu.VMEM((8, 2 * D_FF), F32), pltpu.VMEM((tt, d), F32)]
    out = pl.pallas_call(
        functools.partial(_ffn_body, tt=tt, alpha=alpha),
        grid=(bsz, n_t), in_specs=in_specs,
        out_specs=pl.BlockSpec((1, tt, d), lambda b, t: (b, jnp.maximum(t - n_front, 0), 0)),
        out_shape=jax.ShapeDtypeStruct((bsz, n_tiles_real * tt, d), F32),
        scratch_shapes=scratch, name="ffn",
        compiler_params=pltpu.CompilerParams(dimension_semantics=("arbitrary", "arbitrary"),
                                             vmem_limit_bytes=VMEM_LIMIT),
    )(h1, prm["ffn_up"].astype(BF16), prm["ffn_dw"].astype(F32), prm["ffn_down"].astype(BF16),
      _row2d(prm["ln2_g"]), _row2d(prm["ln2_b"]))
    return out


def kernel(x, meta_tokens, ln_in_g, ln_in_b, w_in, b_gate, conv_dw, conv_dw_b, conv_ln_g, conv_ln_b, w_conv_out, rwkv_mu, w0, w_decay_up, a0, a_up, g_up, k_k, k_a, r_k, lnx_g, lnx_b, w_rwkv_out, w_o, ln1_g, ln1_b, ffn_up, ffn_dw, ffn_down, ln2_g, ln2_b):
    bsz, seq, d = x.shape
    depth = w_in.shape[0]
    assert d == D_MODEL and seq % TT == 0 and depth == 1, (x.shape, w_in.shape)
    alpha = (2.0 * depth) ** 0.25
    consts = _mask_constants()
    meta_pad = jnp.concatenate([jnp.zeros((FRONT - N_META, d), F32), meta_tokens.astype(F32)], axis=0)
    prm = dict(meta_pad=meta_pad, ln_in_g=_row2d(ln_in_g), ln_in_b=_row2d(ln_in_b),
               w_in=w_in[0], b_gate=b_gate[0], conv_dw=conv_dw[0], conv_dw_b=conv_dw_b[0],
               conv_ln_g=conv_ln_g[0], conv_ln_b=conv_ln_b[0], w_conv_out=w_conv_out[0], rwkv_mu=rwkv_mu[0],
               w0=w0[0], w_decay_up=w_decay_up[0], a0=a0[0], a_up=a_up[0], g_up=g_up[0], k_k=k_k[0], k_a=k_a[0],
               r_k=r_k[0], lnx_g=lnx_g[0], lnx_b=lnx_b[0], w_rwkv_out=w_rwkv_out[0], w_o=w_o[0],
               ln1_g=ln1_g[0], ln1_b=ln1_b[0], ffn_up=ffn_up[0], ffn_dw=ffn_dw[0], ffn_down=ffn_down[0],
               ln2_g=ln2_g[0], ln2_b=ln2_b[0])
    return _layer(x.astype(F32), prm, consts, bsz, seq // TT, alpha)
```

```python
import functools
import math

import jax
import jax.numpy as jnp
import numpy as np
from jax import lax
from jax.experimental import pallas as pl
from jax.experimental.pallas import tpu as pltpu

D_MODEL = 1024
N_META = 16
CONV_KERNEL = 31
RWKV_HEAD = 64
RWKV_HEADS = D_MODEL // RWKV_HEAD
DECAY_LORA = 64
ICLR_LORA = 64
GATE_LORA = 160
D_FF = 2816
FFN_KERNEL = 3
LN_EPS = 1e-5
GN_EPS = 64e-5

FRONT = 256
TT_PREP = 128
TT = 256
CHUNK = 64
PAIR = 2 * RWKV_HEAD
N_PAIR = D_MODEL // PAIR
HALO = 32
RB = 16
FB = 256
X_COLS = 384
VMEM_LIMIT = 56 * 1024 * 1024

F32 = jnp.float32
BF16 = jnp.bfloat16


def _dot(a, b, **kw):
    return jnp.dot(a, b, preferred_element_type=F32, **kw)


def _dot_nt(a, b):
    return lax.dot_general(a, b, (((1,), (1,)), ((), ())), preferred_element_type=F32)


def _dot_tn(a, b, **kw):
    return lax.dot_general(a, b, (((0,), (0,)), ((), ())), preferred_element_type=F32, **kw)


def _sigmoid(x):
    return 1.0 / (1.0 + jnp.exp(-x))


def _layer_norm(x, g, b, eps):
    mu = jnp.mean(x, axis=-1, keepdims=True)
    xc = x - mu
    var = jnp.mean(xc * xc, axis=-1, keepdims=True)
    return xc * lax.rsqrt(var + eps) * g + b


def _row_loop(n_rows, rb, fn, unroll=1):
    def body(i, c):
        fn(pl.multiple_of(i * rb, rb))
        return c
    lax.fori_loop(0, n_rows // rb, body, 0, unroll=unroll)


def _seg_sum(xb, g_ref):
    parts = [_dot(xb[:, c * 256:(c + 1) * 256], g_ref[...]) for c in range(D_MODEL // 256)]
    return jnp.concatenate(parts, axis=1)


def _mix_prep_body(x_ref, meta_ref, lng_ref, lnb_ref, wc_ref, wrkv_ref, wx_ref, wgt_ref, bgate_ref,
                   convw_ref, convb_ref, clng_ref, clnb_ref, wco_ref, mu_rkv_ref, mu_x_ref,
                   w0_ref, wdu_ref, a0_ref, wau_ref, wgu_ref, kk_ref, ka_ref, gones_ref,
                   h_ref, r_ref, ld_ref, k2_ref, v_ref, kkn_ref, b_ref, g_ref, ycg_ref, gb_ref,
                   hb_s, pc_s, cbuf_s, conv_s, cact_s, pbuf_s, rkv_s, pg_s, t1_s, x1_s, s2_s,
                   z1_s, z2_s, kk2_s, ss_s, yc_s, *, tt):
    ti = pl.program_id(1)
    n_skip = (FRONT - N_META) // tt
    outs = (h_ref, r_ref, ld_ref, k2_ref, v_ref, kkn_ref, b_ref, g_ref, ycg_ref, gb_ref)

    @pl.when(ti == 0)
    def _():
        cbuf_s[0:HALO, :] = jnp.zeros((HALO, D_MODEL), F32)
        pbuf_s[0:8, :] = jnp.zeros((8, pbuf_s.shape[1]), F32)

    @pl.when(ti < n_skip)
    def _():
        for o in outs:
            o[...] = jnp.zeros(o.shape, o.dtype)

    @pl.when(ti >= n_skip)
    def _():
        _mix_prep_tile(x_ref, meta_ref, lng_ref, lnb_ref, wc_ref, wrkv_ref, wx_ref, wgt_ref, bgate_ref,
                       convw_ref, convb_ref, clng_ref, clnb_ref, wco_ref, mu_rkv_ref, mu_x_ref,
                       w0_ref, wdu_ref, a0_ref, wau_ref, wgu_ref, kk_ref, ka_ref, gones_ref, *outs,
                       hb_s, pc_s, cbuf_s, conv_s, cact_s, pbuf_s, rkv_s, pg_s, t1_s, x1_s, s2_s,
                       z1_s, z2_s, kk2_s, ss_s, yc_s, tt=tt)


def _mix_prep_tile(x_ref, meta_ref, lng_ref, lnb_ref, wc_ref, wrkv_ref, wx_ref, wgt_ref, bgate_ref,
                   convw_ref, convb_ref, clng_ref, clnb_ref, wco_ref, mu_rkv_ref, mu_x_ref,
                   w0_ref, wdu_ref, a0_ref, wau_ref, wgu_ref, kk_ref, ka_ref, gones_ref,
                   h_ref, r_ref, ld_ref, k2_ref, v_ref, kkn_ref, b_ref, g_ref, ycg_ref, gb_ref,
                   hb_s, pc_s, cbuf_s, conv_s, cact_s, pbuf_s, rkv_s, pg_s, t1_s, x1_s, s2_s,
                   z1_s, z2_s, kk2_s, ss_s, yc_s, *, tt):
    ti = pl.program_id(1)
    n_front = FRONT // tt
    first_real = FRONT - N_META
    n_rkv = 3 * D_MODEL

    def ln_rows(base):
        rows = pl.ds(base, RB)
        xin = jnp.where(ti < n_front, meta_ref[rows, :], x_ref[0, rows, :])
        h = _layer_norm(xin, lng_ref[...], lnb_ref[...], LN_EPS)
        grow = ti * tt + base + lax.broadcasted_iota(jnp.int32, (RB, 1), 0)
        h = jnp.where(grow >= first_real, h, 0.0)
        h_ref[0, rows, :] = h
        hb_s[rows, :] = h.astype(BF16)
    _row_loop(tt, RB, ln_rows, unroll=4)

    hb = hb_s[...]
    pc_s[...] = _dot(hb, wc_ref[...])
    pbuf_s[8:8 + tt, 0:n_rkv] = _dot(hb, wrkv_ref[...])
    pbuf_s[8:8 + tt, n_rkv:n_rkv + X_COLS] = _dot(hb, wx_ref[...])
    pg_s[...] = _dot(hb, wgt_ref[...]) + bgate_ref[...]

    for base in range(0, tt, RB):
        rows = slice(base, base + RB)
        c = pc_s[rows, 0:D_MODEL] * _sigmoid(pc_s[rows, D_MODEL:2 * D_MODEL])
        cbuf_s[base + HALO:base + HALO + RB, :] = c

    cr = 64
    for base in range(0, tt, cr):
        for cb in range(D_MODEL // 128):
            cs = slice(cb * 128, (cb + 1) * 128)
            win = cbuf_s[base:base + cr + HALO, cs]
            acc = jnp.zeros((cr, 128), F32) + convb_ref[:, cs]
            for r in range(8):
                sh = win if r == 0 else pltpu.roll(win, r, axis=0)
                for q in range(4):
                    d = 8 * q + r
                    if d > CONV_KERNEL - 1:
                        continue
                    j = CONV_KERNEL - 1 - d
                    acc = acc + convw_ref[j:j + 1, cs] * sh[HALO - 8 * q:HALO - 8 * q + cr]
            conv_s[base:base + cr, cs] = acc
    cbuf_s[0:HALO, :] = cbuf_s[tt:tt + HALO, :]

    def lerp(base, cols, mu):
        win = pbuf_s[base:base + RB + 8, cols]
        cur = win[8:8 + RB]
        prev = pltpu.roll(win, 1, axis=0)[8:8 + RB]
        return cur + (prev - cur) * mu

    for base in range(0, tt, RB):
        rows = slice(base, base + RB)
        for c in range(n_rkv // 256):
            cols = slice(c * 256, (c + 1) * 256)
            rkv_s[rows, cols] = lerp(base, cols, mu_rkv_ref[:, cols])
        x1 = lerp(base, slice(n_rkv, n_rkv + 128), mu_x_ref[:, 0:128])
        x2 = lerp(base, slice(n_rkv + 128, n_rkv + X_COLS), mu_x_ref[:, 128:X_COLS])
        t1_s[rows, :] = jnp.tanh(x1).astype(BF16)
        x1_s[rows, :] = x1.astype(BF16)
        s2_s[rows, :] = _sigmoid(x2).astype(BF16)
        kk = rkv_s[rows, D_MODEL:2 * D_MODEL] * kk_ref[...]
        kk2_s[rows, :] = (kk * kk).astype(BF16)
    pbuf_s[0:8, :] = pbuf_s[tt:tt + 8, :]

    z1_s[...] = _dot(t1_s[...], wdu_ref[...])
    z2_s[...] = _dot(x1_s[...], wau_ref[...])
    g_ref[0] = _dot(s2_s[...], wgu_ref[...]).astype(g_ref.dtype)
    ss_s[...] = _seg_sum(kk2_s[...], gones_ref)

    for base in range(0, tt, RB):
        rows = slice(base, base + RB)
        y = _layer_norm(conv_s[rows, :], clng_ref[...], clnb_ref[...], LN_EPS)
        cact_s[rows, :] = (y * _sigmoid(y)).astype(BF16)
    yc_s[...] = _dot(cact_s[...], wco_ref[...])

    for base in range(0, tt, RB):
        rows = slice(base, base + RB)
        r = rkv_s[rows, 0:D_MODEL]
        k = rkv_s[rows, D_MODEL:2 * D_MODEL]
        v = rkv_s[rows, 2 * D_MODEL:3 * D_MODEL]
        ld = -math.exp(-0.5) * _sigmoid(w0_ref[...] + z1_s[rows, :])
        a = _sigmoid(a0_ref[...] + z2_s[rows, :])
        kkn = (k * kk_ref[...]) * jnp.minimum(lax.rsqrt(ss_s[rows, :]), 1e12)
        r_ref[0, rows, :] = r
        v_ref[0, rows, :] = v
        ld_ref[0, rows, :] = ld
        k2_ref[0, rows, :] = k * (1.0 + (a - 1.0) * ka_ref[...])
        kkn_ref[0, rows, :] = kkn
        b_ref[0, rows, :] = kkn * a

    for base in range(0, tt, RB):
        rows = slice(base, base + RB)
        ycg_ref[0, rows, :] = (_sigmoid(pg_s[rows, 0:D_MODEL]) * yc_s[rows, :]).astype(ycg_ref.dtype)
        gb_ref[0, rows, :] = _sigmoid(pg_s[rows, D_MODEL:2 * D_MODEL]).astype(gb_ref.dtype)


def _block_diag(yb, bdmask):
    y2 = jnp.concatenate([yb, yb], axis=0)
    return jnp.where(bdmask, y2, jnp.zeros_like(y2))


def _chunk_prep(ins, consts):
    ltri, eye2, trimask, bdmask, bdmask32, lane0, ones_c = consts
    pairs = range(len(ins))
    ldc, rc, kc, vc, kknc, bc = ([x[i] for x in ins] for i in range(6))
    ld_hi = [ldc[p].astype(BF16) for p in pairs]
    ld_lo = [(ldc[p] - ld_hi[p].astype(F32)).astype(BF16) for p in pairs]
    cum = [_dot(ltri, ld_hi[p]) + _dot(ltri, ld_lo[p]) for p in pairs]
    wl_col = [jnp.exp(_dot_tn(ld_hi[p], ones_c) + _dot_tn(ld_lo[p], ones_c)) for p in pairs]
    atb, rtb, ys, bw, kw = [], [], [], [], []
    for p in pairs:
        w_inv = jnp.exp(-cum[p])
        bt = (bc[p] * w_inv).astype(BF16)
        kt = (kc[p] * w_inv).astype(BF16)
        zero = jnp.zeros_like(bt)
        ys.append(jnp.concatenate([jnp.where(lane0, bt, zero), jnp.where(lane0, zero, bt),
                                   jnp.where(lane0, kt, zero), jnp.where(lane0, zero, kt)], axis=0))
        atb.append((-(kknc[p] * jnp.exp(cum[p] - ldc[p]))).astype(BF16))
        rtb.append((rc[p] * jnp.exp(cum[p])).astype(BF16))
        w_rest = jnp.exp(cum[p][CHUNK - 1:CHUNK, :] - cum[p])
        bw.append((bc[p] * w_rest).astype(BF16))
        kw.append((kc[p] * w_rest).astype(BF16))
    sc = [jnp.where(trimask, _dot_nt(jnp.concatenate([atb[p], rtb[p]], axis=0), ys[p]), 0.0) for p in pairs]
    a_ab = [sc[p][0:CHUNK, 0:PAIR] for p in pairs]
    a_akb = [sc[p][0:CHUNK, PAIR:2 * PAIR].astype(BF16) for p in pairs]
    r_bb = [sc[p][CHUNK:2 * CHUNK, 0:PAIR].astype(BF16) for p in pairs]
    r_kb = [sc[p][CHUNK:2 * CHUNK, PAIR:2 * PAIR].astype(BF16) for p in pairs]

    n = [_dot(a_ab[p].astype(BF16), _block_diag(a_ab[p].astype(BF16), bdmask)) for p in pairs]
    t = [eye2 + a_ab[p] for p in pairs]
    for _ in range(4):
        res = [_dot(jnp.concatenate([n[p], t[p]], axis=0).astype(BF16), _block_diag(n[p].astype(BF16), bdmask))
               for p in pairs]
        n = [res[p][0:CHUNK] for p in pairs]
        t = [t[p] + res[p][CHUNK:2 * CHUNK] for p in pairs]
    tb = [(t[p] + _dot(t[p].astype(BF16), _block_diag(n[p].astype(BF16), bdmask))).astype(BF16) for p in pairs]
    vb = [vc[p].astype(BF16) for p in pairs]
    v_bd = [_block_diag(vb[p], bdmask) for p in pairs]
    return [dict(tb=tb[p], atb=atb[p], rtb=rtb[p], a_akb=a_akb[p], r_bb=r_bb[p], r_kb=r_kb[p], bw=bw[p], kw=kw[p],
                 vb=vb[p], v_bd=v_bd[p], wl_col=wl_col[p]) for p in pairs]


def _chunk_apply(pre, m_bd, consts):
    bdmask, bdmask32 = consts[3], consts[4]
    pairs = range(len(pre))
    mb = [m_bd[p].astype(BF16) for p in pairs]
    q = [_dot(jnp.concatenate([pre[p]["atb"], pre[p]["a_akb"]], axis=1),
              jnp.concatenate([mb[p], pre[p]["v_bd"]], axis=0)) for p in pairs]
    ub = [_dot(pre[p]["tb"], _block_diag(q[p].astype(BF16), bdmask)).astype(BF16) for p in pairs]
    y = [_dot(jnp.concatenate([pre[p]["rtb"], pre[p]["r_bb"], pre[p]["r_kb"]], axis=1),
              jnp.concatenate([mb[p], _block_diag(ub[p], bdmask), pre[p]["v_bd"]], axis=0)) for p in pairs]
    upd = [_dot_tn(jnp.concatenate([pre[p]["bw"], pre[p]["kw"]], axis=0),
                   jnp.concatenate([ub[p], pre[p]["vb"]], axis=0)) for p in pairs]
    m_new = [pre[p]["wl_col"] * m_bd[p] + jnp.where(bdmask32, upd[p], 0.0) for p in pairs]
    return y, m_new


def _mix_core_body(r_ref, ld_ref, k2_ref, v_ref, kkn_ref, b_ref, g_ref, ycg_ref, gb_ref, h_ref,
                   ltri_ref, eye2_ref, trimask_ref, bdmask_ref, bdmask32_ref, lane0_ref, gones_ref, gmean_ref,
                   rk_ref, lnxg_ref, lnxb_ref, wro_ref, wo_ref, ln1g_ref, ln1b_ref,
                   h1_ref, m_s, o_s, ob_s, t32_s, rk_s, bon_s, *, tt, alpha):
    ti = pl.program_id(1)

    @pl.when(ti == 0)
    def _():
        m_s[...] = jnp.zeros(m_s.shape, F32)

    consts = (ltri_ref[...], eye2_ref[...], trimask_ref[...] > 0.5, bdmask_ref[...] > 0.5,
              bdmask32_ref[...] > 0.5, lane0_ref[...] > 0.5, jnp.ones((CHUNK, PAIR), BF16))

    n_chunk = tt // CHUNK
    cols = [slice(p * PAIR, (p + 1) * PAIR) for p in range(N_PAIR)]
    ins = []
    for c in range(n_chunk):
        rows = slice(c * CHUNK, (c + 1) * CHUNK)
        ins += [(ld_ref[0, rows, cs], r_ref[0, rows, cs], k2_ref[0, rows, cs], v_ref[0, rows, cs],
                 kkn_ref[0, rows, cs], b_ref[0, rows, cs]) for cs in cols]
    pre = _chunk_prep(ins, consts)
    m = [m_s[p] for p in range(N_PAIR)]
    for c in range(n_chunk):
        y, m = _chunk_apply(pre[c * N_PAIR:(c + 1) * N_PAIR], m, consts)
        for p in range(N_PAIR):
            o_s[c * CHUNK:(c + 1) * CHUNK, cols[p]] = y[p]
    for p in range(N_PAIR):
        m_s[p] = m[p]

    blocks = [slice(base, base + RB) for base in range(0, tt, RB)]
    for rows in blocks:
        ob_s[rows, :] = o_s[rows, :].astype(BF16)
    t32_s[...] = _seg_sum(ob_s[...], gmean_ref)
    for rows in blocks:
        xc = o_s[rows, :] - t32_s[rows, :]
        o_s[rows, :] = xc
        ob_s[rows, :] = (xc * xc).astype(BF16)
        rk_s[rows, :] = (r_ref[0, rows, :] * k2_ref[0, rows, :] * rk_ref[...]).astype(BF16)
    t32_s[...] = _seg_sum(ob_s[...], gmean_ref)
    bon_s[...] = _seg_sum(rk_s[...], gones_ref)
    for rows in blocks:
        on = o_s[rows, :] * lax.rsqrt(t32_s[rows, :] + GN_EPS) * lnxg_ref[...] + lnxb_ref[...]
        o2 = (on + bon_s[rows, :] * v_ref[0, rows, :]) * g_ref[0, rows, :].astype(F32)
        ob_s[rows, :] = o2.astype(BF16)
    t32_s[...] = _dot(ob_s[...], wro_ref[...])
    for rows in blocks:
        y = ycg_ref[0, rows, :].astype(F32) + gb_ref[0, rows, :].astype(F32) * t32_s[rows, :]
        ob_s[rows, :] = y.astype(BF16)
    t32_s[...] = _dot(ob_s[...], wo_ref[...])
    for rows in blocks:
        h1_ref[0, rows, :] = _layer_norm(alpha * h_ref[0, rows, :] + t32_s[rows, :],
                                         ln1g_ref[...], ln1b_ref[...], LN_EPS)


def _ffn_body(h1_ref, wup_ref, dw_ref, wdn_ref, ln2g_ref, ln2b_ref, out_ref,
              hb_s, carry_s, y_s, *, tt, alpha):
    ti = pl.program_id(1)
    first_real = FRONT - N_META

    @pl.when(ti == 0)
    def _():
        carry_s[...] = jnp.zeros(carry_s.shape, F32)

    for base in range(0, tt, RB):
        rows = slice(base, base + RB)
        grow = ti * tt + base + lax.broadcasted_iota(jnp.int32, (RB, 1), 0)
        hb_s[rows, :] = jnp.where(grow >= first_real, h1_ref[0, rows, :], 0.0).astype(BF16)

    n_blk = D_FF // FB

    def up(j):
        hb = hb_s[...]
        return [_dot(hb, wup_ref[:, half * D_FF + j * FB:half * D_FF + (j + 1) * FB]) for half in (0, 1)]

    def conv(cols, u):
        ext = jnp.concatenate([carry_s[:, cols], u], axis=0)
        carry_s[:, cols] = u[tt - 8:tt]
        return (dw_ref[2:3, cols] * u + dw_ref[1:2, cols] * pltpu.roll(ext, 1, axis=0)[8:8 + tt]
                + dw_ref[0:1, cols] * pltpu.roll(ext, 2, axis=0)[8:8 + tt])

    def act(j, u):
        lin = conv(slice(j * FB, (j + 1) * FB), u[0])
        gate = conv(slice(D_FF + j * FB, D_FF + (j + 1) * FB), u[1])
        return (gate * _sigmoid(gate) * lin).astype(BF16)

    u_next = up(0)
    for j in range(n_blk):
        u = u_next
        if j + 1 < n_blk:
            u_next = up(j + 1)
        d = _dot(act(j, u), wdn_ref[j * FB:(j + 1) * FB, :])
        if j == 0:
            y_s[...] = d
        else:
            y_s[...] += d

    for base in range(0, tt, RB):
        rows = slice(base, base + RB)
        out_ref[0, rows, :] = _layer_norm(alpha * h1_ref[0, rows, :] + y_s[rows, :],
                                          ln2g_ref[...], ln2b_ref[...], LN_EPS)


def _const(shape):
    nd = len(shape)
    return pl.BlockSpec(shape, lambda b, t: (0,) * nd, pipeline_mode=pl.Buffered(1))


def _row2d(v):
    return v.reshape(1, -1).astype(F32)


def _mask_constants():
    t = np.arange(2 * CHUNK)[:, None] % CHUNK
    s = np.arange(2 * PAIR)[None, :] % RWKV_HEAD
    strict = (t > s)
    incl = (t >= s)
    trimask = np.where(np.arange(2 * CHUNK)[:, None] < CHUNK, strict, incl).astype(np.float32)
    hh = np.arange(PAIR) // RWKV_HEAD
    bdmask = (hh[:, None] == hh[None, :]).astype(np.float32)
    ltri = np.tril(np.ones((CHUNK, CHUNK), np.float32))
    eye2 = np.concatenate([np.eye(CHUNK, dtype=np.float32)] * 2, axis=1)
    lane0 = np.broadcast_to((np.arange(PAIR) < RWKV_HEAD)[None, :], (CHUNK, PAIR)).astype(np.float32)
    h4 = np.arange(256) // RWKV_HEAD
    gones = (h4[:, None] == h4[None, :]).astype(np.float32)
    return dict(ltri=jnp.asarray(ltri, BF16), eye2=jnp.asarray(eye2), trimask=jnp.asarray(trimask),
                bdmask=jnp.asarray(bdmask, BF16), bdmask32=jnp.asarray(bdmask), lane0=jnp.asarray(lane0, BF16),
                gones=jnp.asarray(gones, BF16), gmean=jnp.asarray(gones / RWKV_HEAD, BF16))


def _layer(x, prm, consts, bsz, n_tiles_real, alpha):
    d = D_MODEL
    tp = FRONT + n_tiles_real * TT
    c_end = 2 * d
    i_r = c_end
    i_x = i_r + 3 * d
    r_end = i_x + DECAY_LORA + ICLR_LORA + GATE_LORA
    w_in = prm["w_in"]
    wc = w_in[:, :c_end].astype(BF16)
    wrkv = w_in[:, i_r:i_x].astype(BF16)
    wx = jnp.pad(w_in[:, i_x:r_end], ((0, 0), (0, X_COLS - (r_end - i_x)))).astype(BF16)
    wgt = w_in[:, r_end:].astype(BF16)
    mu = prm["rwkv_mu"]
    mu_rkv = _row2d(mu[:3 * d])
    mu_x = _row2d(jnp.pad(mu[3 * d:], (0, X_COLS - (r_end - i_x))))
    wdu = jnp.pad(prm["w_decay_up"], ((0, 128 - DECAY_LORA), (0, 0))).astype(BF16)
    wau = jnp.pad(prm["a_up"], ((DECAY_LORA, 0), (0, 0))).astype(BF16)
    wgu = jnp.pad(prm["g_up"], ((0, 256 - GATE_LORA), (0, 0))).astype(BF16)

    tt = TT_PREP
    n_front = FRONT // tt
    n_t = tp // tt
    tile = pl.BlockSpec((1, tt, d), lambda b, t: (b, t, 0))
    in_specs = [
        pl.BlockSpec((1, tt, d), lambda b, t: (b, jnp.maximum(t - n_front, 0), 0)),
        pl.BlockSpec((tt, d), lambda b, t: (jnp.minimum(t, n_front - 1), 0)),
        _const((1, d)), _const((1, d)),
        _const((d, 2 * d)), _const((d, 3 * d)), _const((d, X_COLS)), _const((d, 2 * d)), _const((1, 2 * d)),
        _const((CONV_KERNEL, d)), _const((1, d)), _const((1, d)), _const((1, d)), _const((d, d)),
        _const((1, 3 * d)), _const((1, X_COLS)),
        _const((1, d)), _const((128, d)), _const((1, d)), _const((128, d)), _const((256, d)),
        _const((1, d)), _const((1, d)), _const((256, 256)),
    ]
    act = jax.ShapeDtypeStruct((bsz, tp, d), F32)
    act_bf = jax.ShapeDtypeStruct((bsz, tp, d), BF16)
    scratch = [
        pltpu.VMEM((tt, d), BF16),
        pltpu.VMEM((tt, 2 * d), F32),
        pltpu.VMEM((tt + HALO, d), F32),
        pltpu.VMEM((tt, d), F32),
        pltpu.VMEM((tt, d), BF16),
        pltpu.VMEM((tt + 8, 3 * d + X_COLS), F32),
        pltpu.VMEM((tt, 3 * d), F32),
        pltpu.VMEM((tt, 2 * d), F32),
        pltpu.VMEM((tt, 128), BF16), pltpu.VMEM((tt, 128), BF16), pltpu.VMEM((tt, 256), BF16),
        pltpu.VMEM((tt, d), F32), pltpu.VMEM((tt, d), F32), pltpu.VMEM((tt, d), BF16),
        pltpu.VMEM((tt, d), F32), pltpu.VMEM((tt, d), F32),
    ]
    outs = pl.pallas_call(
        functools.partial(_mix_prep_body, tt=tt),
        grid=(bsz, n_t), in_specs=in_specs, out_specs=[tile] * 10, out_shape=[act] * 7 + [act_bf] * 3,
        scratch_shapes=scratch, name="mix_prep",
        compiler_params=pltpu.CompilerParams(dimension_semantics=("arbitrary", "arbitrary"),
                                             vmem_limit_bytes=VMEM_LIMIT),
    )(x, prm["meta_pad"], prm["ln_in_g"], prm["ln_in_b"], wc, wrkv, wx, wgt, _row2d(prm["b_gate"]),
      prm["conv_dw"].astype(F32), _row2d(prm["conv_dw_b"]), _row2d(prm["conv_ln_g"]), _row2d(prm["conv_ln_b"]),
      prm["w_conv_out"].astype(BF16), mu_rkv, mu_x,
      _row2d(prm["w0"]), wdu, _row2d(prm["a0"]), wau, wgu, _row2d(prm["k_k"]), _row2d(prm["k_a"]),
      consts["gones"])
    h, r, ld, k2, v, kkn, b, g, ycg, gb = outs

    tt = TT
    n_t = tp // tt
    tile = pl.BlockSpec((1, tt, d), lambda b, t: (b, t, 0))
    in_specs = [tile] * 10 + [
        _const((CHUNK, CHUNK)), _const((CHUNK, PAIR)), _const((2 * CHUNK, 2 * PAIR)), _const((PAIR, PAIR)),
        _const((PAIR, PAIR)),
        _const((CHUNK, PAIR)), _const((256, 256)), _const((256, 256)),
        _const((1, d)), _const((1, d)), _const((1, d)), _const((d, d)), _const((d, d)), _const((1, d)), _const((1, d)),
    ]
    scratch = [
        pltpu.VMEM((N_PAIR, PAIR, PAIR), F32),
        pltpu.VMEM((tt, d), F32), pltpu.VMEM((tt, d), BF16), pltpu.VMEM((tt, d), F32),
        pltpu.VMEM((tt, d), BF16), pltpu.VMEM((tt, d), F32),
    ]
    h1 = pl.pallas_call(
        functools.partial(_mix_core_body, tt=tt, alpha=alpha),
        grid=(bsz, n_t), in_specs=in_specs, out_specs=tile, out_shape=act,
        scratch_shapes=scratch, name="mix_core",
        compiler_params=pltpu.CompilerParams(dimension_semantics=("arbitrary", "arbitrary"),
                                             vmem_limit_bytes=VMEM_LIMIT),
    )(r, ld, k2, v, kkn, b, g, ycg, gb, h,
      consts["ltri"], consts["eye2"], consts["trimask"], consts["bdmask"], consts["bdmask32"], consts["lane0"],
      consts["gones"], consts["gmean"],
      _row2d(prm["r_k"]), _row2d(prm["lnx_g"]), _row2d(prm["lnx_b"]),
      prm["w_rwkv_out"].astype(BF16), prm["w_o"].astype(BF16), _row2d(prm["ln1_g"]), _row2d(prm["ln1_b"]))

    n_front = FRONT // tt
    in_specs = [tile, _const((d, 2 * D_FF)), _const((FFN_KERNEL, 2 * D_FF)), _const((D_FF, d)),
                _const((1, d)), _const((1, d))]
    scratch = [pltpu.VMEM((tt, d), BF16), pltpu.VMEM((8, 2 * D_FF), F32), pltpu.VMEM((tt, d), F32)]
    out = pl.pallas_call(
        functools.partial(_ffn_body, tt=tt, alpha=alpha),
        grid=(bsz, n_t), in_specs=in_specs,
        out_specs=pl.BlockSpec((1, tt, d), lambda b, t: (b, jnp.maximum(t - n_front, 0), 0)),
        out_shape=jax.ShapeDtypeStruct((bsz, n_tiles_real * tt, d), F32),
        scratch_shapes=scratch, name="ffn",
        compiler_params=pltpu.CompilerParams(dimension_semantics=("arbitrary", "arbitrary"),
                                             vmem_limit_bytes=VMEM_LIMIT),
    )(h1, prm["ffn_up"].astype(BF16), prm["ffn_dw"].astype(F32), prm["ffn_down"].astype(BF16),
      _row2d(prm["ln2_g"]), _row2d(prm["ln2_b"]))
    return out


def kernel(x, meta_tokens, ln_in_g, ln_in_b, w_in, b_gate, conv_dw, conv_dw_b, conv_ln_g, conv_ln_b, w_conv_out, rwkv_mu, w0, w_decay_up, a0, a_up, g_up, k_k, k_a, r_k, lnx_g, lnx_b, w_rwkv_out, w_o, ln1_g, ln1_b, ffn_up, ffn_dw, ffn_down, ln2_g, ln2_b):
    bsz, seq, d = x.shape
    depth = w_in.shape[0]
    assert d == D_MODEL and seq % TT == 0 and depth == 1, (x.shape, w_in.shape)
    alpha = (2.0 * depth) ** 0.25
    consts = _mask_constants()
    meta_pad = jnp.concatenate([jnp.zeros((FRONT - N_META, d), F32), meta_tokens.astype(F32)], axis=0)
    prm = dict(meta_pad=meta_pad, ln_in_g=_row2d(ln_in_g), ln_in_b=_row2d(ln_in_b),
               w_in=w_in[0], b_gate=b_gate[0], conv_dw=conv_dw[0], conv_dw_b=conv_dw_b[0],
               conv_ln_g=conv_ln_g[0], conv_ln_b=conv_ln_b[0], w_conv_out=w_conv_out[0], rwkv_mu=rwkv_mu[0],
               w0=w0[0], w_decay_up=w_decay_up[0], a0=a0[0], a_up=a_up[0], g_up=g_up[0], k_k=k_k[0], k_a=k_a[0],
               r_k=r_k[0], lnx_g=lnx_g[0], lnx_b=lnx_b[0], w_rwkv_out=w_rwkv_out[0], w_o=w_o[0],
               ln1_g=ln1_g[0], ln1_b=ln1_b[0], ffn_up=ffn_up[0], ffn_dw=ffn_dw[0], ffn_down=ffn_down[0],
               ln2_g=ln2_g[0], ln2_b=ln2_b[0])
    return _layer(x.astype(F32), prm, consts, bsz, seq // TT, alpha)
```

```python
import functools
import math

import jax
import jax.numpy as jnp
import numpy as np
from jax import lax
from jax.experimental import pallas as pl
from jax.experimental.pallas import tpu as pltpu

D_MODEL = 1024
N_META = 16
CONV_KERNEL = 31
RWKV_HEAD = 64
RWKV_HEADS = D_MODEL // RWKV_HEAD
DECAY_LORA = 64
ICLR_LORA = 64
GATE_LORA = 160
D_FF = 2816
FFN_KERNEL = 3
LN_EPS = 1e-5
GN_EPS = 64e-5

FRONT = 256
TT_PREP = 128
TT = 256
CHUNK = 64
PAIR = 2 * RWKV_HEAD
N_PAIR = D_MODEL // PAIR
HALO = 32
RB = 16
FB = 256
X_COLS = 384
VMEM_LIMIT = 56 * 1024 * 1024

F32 = jnp.float32
BF16 = jnp.bfloat16


def _dot(a, b, **kw):
    return jnp.dot(a, b, preferred_element_type=F32, **kw)


def _dot_nt(a, b):
    return lax.dot_general(a, b, (((1,), (1,)), ((), ())), preferred_element_type=F32)


def _dot_tn(a, b, **kw):
    return lax.dot_general(a, b, (((0,), (0,)), ((), ())), preferred_element_type=F32, **kw)


def _sigmoid(x):
    return 1.0 / (1.0 + jnp.exp(-x))


def _layer_norm(x, g, b, eps):
    mu = jnp.mean(x, axis=-1, keepdims=True)
    xc = x - mu
    var = jnp.mean(xc * xc, axis=-1, keepdims=True)
    return xc * lax.rsqrt(var + eps) * g + b


def _row_loop(n_rows, rb, fn, unroll=1):
    def body(i, c):
        fn(pl.multiple_of(i * rb, rb))
        return c
    lax.fori_loop(0, n_rows // rb, body, 0, unroll=unroll)


def _seg_sum(xb, g_ref):
    parts = [_dot(xb[:, c * 256:(c + 1) * 256], g_ref[...]) for c in range(D_MODEL // 256)]
    return jnp.concatenate(parts, axis=1)


def _mix_prep_body(x_ref, meta_ref, lng_ref, lnb_ref, wc_ref, wrkv_ref, wx_ref, wgt_ref, bgate_ref,
                   convw_ref, convb_ref, clng_ref, clnb_ref, wco_ref, mu_rkv_ref, mu_x_ref,
                   w0_ref, wdu_ref, a0_ref, wau_ref, wgu_ref, kk_ref, ka_ref, gones_ref,
                   h_ref, r_ref, ld_ref, k2_ref, v_ref, kkn_ref, b_ref, g_ref, ycg_ref, gb_ref,
                   hb_s, pc_s, cbuf_s, conv_s, cact_s, pbuf_s, rkv_s, pg_s, t1_s, x1_s, s2_s,
                   z1_s, z2_s, kk2_s, ss_s, yc_s, *, tt):
    ti = pl.program_id(1)
    n_skip = (FRONT - N_META) // tt
    outs = (h_ref, r_ref, ld_ref, k2_ref, v_ref, kkn_ref, b_ref, g_ref, ycg_ref, gb_ref)

    @pl.when(ti == 0)
    def _():
        cbuf_s[0:HALO, :] = jnp.zeros((HALO, D_MODEL), F32)
        pbuf_s[0:8, :] = jnp.zeros((8, pbuf_s.shape[1]), F32)

    @pl.when(ti < n_skip)
    def _():
        for o in outs:
            o[...] = jnp.zeros(o.shape, o.dtype)

    @pl.when(ti >= n_skip)
    def _():
        _mix_prep_tile(x_ref, meta_ref, lng_ref, lnb_ref, wc_ref, wrkv_ref, wx_ref, wgt_ref, bgate_ref,
                       convw_ref, convb_ref, clng_ref, clnb_ref, wco_ref, mu_rkv_ref, mu_x_ref,
                       w0_ref, wdu_ref, a0_ref, wau_ref, wgu_ref, kk_ref, ka_ref, gones_ref, *outs,
                       hb_s, pc_s, cbuf_s, conv_s, cact_s, pbuf_s, rkv_s, pg_s, t1_s, x1_s, s2_s,
                       z1_s, z2_s, kk2_s, ss_s, yc_s, tt=tt)


def _mix_prep_tile(x_ref, meta_ref, lng_ref, lnb_ref, wc_ref, wrkv_ref, wx_ref, wgt_ref, bgate_ref,
                   convw_ref, convb_ref, clng_ref, clnb_ref, wco_ref, mu_rkv_ref, mu_x_ref,
                   w0_ref, wdu_ref, a0_ref, wau_ref, wgu_ref, kk_ref, ka_ref, gones_ref,
                   h_ref, r_ref, ld_ref, k2_ref, v_ref, kkn_ref, b_ref, g_ref, ycg_ref, gb_ref,
                   hb_s, pc_s, cbuf_s, conv_s, cact_s, pbuf_s, rkv_s, pg_s, t1_s, x1_s, s2_s,
                   z1_s, z2_s, kk2_s, ss_s, yc_s, *, tt):
    ti = pl.program_id(1)
    n_front = FRONT // tt
    first_real = FRONT - N_META
    n_rkv = 3 * D_MODEL

    def ln_rows(base):
        rows = pl.ds(base, RB)
        xin = jnp.where(ti < n_front, meta_ref[rows, :], x_ref[0, rows, :])
        h = _layer_norm(xin, lng_ref[...], lnb_ref[...], LN_EPS)
        grow = ti * tt + base + lax.broadcasted_iota(jnp.int32, (RB, 1), 0)
        h = jnp.where(grow >= first_real, h, 0.0)
        h_ref[0, rows, :] = h
        hb_s[rows, :] = h.astype(BF16)
    _row_loop(tt, RB, ln_rows, unroll=4)

    hb = hb_s[...]
    pc_s[...] = _dot(hb, wc_ref[...])
    pbuf_s[8:8 + tt, 0:n_rkv] = _dot(hb, wrkv_ref[...])
    pbuf_s[8:8 + tt, n_rkv:n_rkv + X_COLS] = _dot(hb, wx_ref[...])
    pg_s[...] = _dot(hb, wgt_ref[...]) + bgate_ref[...]

    for base in range(0, tt, RB):
        rows = slice(base, base + RB)
        c = pc_s[rows, 0:D_MODEL] * _sigmoid(pc_s[rows, D_MODEL:2 * D_MODEL])
        cbuf_s[base + HALO:base + HALO + RB, :] = c

    cr = 64
    for base in range(0, tt, cr):
        for cb in range(D_MODEL // 128):
            cs = slice(cb * 128, (cb + 1) * 128)
            win = cbuf_s[base:base + cr + HALO, cs]
            acc = jnp.zeros((cr, 128), F32) + convb_ref[:, cs]
            for r in range(8):
                sh = win if r == 0 else pltpu.roll(win, r, axis=0)
                for q in range(4):
                    d = 8 * q + r
                    if d > CONV_KERNEL - 1:
                        continue
                    j = CONV_KERNEL - 1 - d
                    acc = acc + convw_ref[j:j + 1, cs] * sh[HALO - 8 * q:HALO - 8 * q + cr]
            conv_s[base:base + cr, cs] = acc
    cbuf_s[0:HALO, :] = cbuf_s[tt:tt + HALO, :]

    def lerp(base, cols, mu):
        win = pbuf_s[base:base + RB + 8, cols]
        cur = win[8:8 + RB]
        prev = pltpu.roll(win, 1, axis=0)[8:8 + RB]
        return cur + (prev - cur) * mu

    for base in range(0, tt, RB):
        rows = slice(base, base + RB)
        for c in range(n_rkv // 256):
            cols = slice(c * 256, (c + 1) * 256)
            rkv_s[rows, cols] = lerp(base, cols, mu_rkv_ref[:, cols])
        x1 = lerp(base, slice(n_rkv, n_rkv + 128), mu_x_ref[:, 0:128])
        x2 = lerp(base, slice(n_rkv + 128, n_rkv + X_COLS), mu_x_ref[:, 128:X_COLS])
        t1_s[rows, :] = jnp.tanh(x1).astype(BF16)
        x1_s[rows, :] = x1.astype(BF16)
        s2_s[rows, :] = _sigmoid(x2).astype(BF16)
        kk = rkv_s[rows, D_MODEL:2 * D_MODEL] * kk_ref[...]
        kk2_s[rows, :] = (kk * kk).astype(BF16)
    pbuf_s[0:8, :] = pbuf_s[tt:tt + 8, :]

    z1_s[...] = _dot(t1_s[...], wdu_ref[...])
    z2_s[...] = _dot(x1_s[...], wau_ref[...])
    g_ref[0] = _dot(s2_s[...], wgu_ref[...]).astype(g_ref.dtype)
    ss_s[...] = _seg_sum(kk2_s[...], gones_ref)

    for base in range(0, tt, RB):
        rows = slice(base, base + RB)
        y = _layer_norm(conv_s[rows, :], clng_ref[...], clnb_ref[...], LN_EPS)
        cact_s[rows, :] = (y * _sigmoid(y)).astype(BF16)
    yc_s[...] = _dot(cact_s[...], wco_ref[...])

    for base in range(0, tt, RB):
        rows = slice(base, base + RB)
        r = rkv_s[rows, 0:D_MODEL]
        k = rkv_s[rows, D_MODEL:2 * D_MODEL]
        v = rkv_s[rows, 2 * D_MODEL:3 * D_MODEL]
        ld = -math.exp(-0.5) * _sigmoid(w0_ref[...] + z1_s[rows, :])
        a = _sigmoid(a0_ref[...] + z2_s[rows, :])
        kkn = (k * kk_ref[...]) * jnp.minimum(lax.rsqrt(ss_s[rows, :]), 1e12)
        r_ref[0, rows, :] = r
        v_ref[0, rows, :] = v
        ld_ref[0, rows, :] = ld
        k2_ref[0, rows, :] = k * (1.0 + (a - 1.0) * ka_ref[...])
        kkn_ref[0, rows, :] = kkn
        b_ref[0, rows, :] = kkn * a

    for base in range(0, tt, RB):
        rows = slice(base, base + RB)
        ycg_ref[0, rows, :] = (_sigmoid(pg_s[rows, 0:D_MODEL]) * yc_s[rows, :]).astype(ycg_ref.dtype)
        gb_ref[0, rows, :] = _sigmoid(pg_s[rows, D_MODEL:2 * D_MODEL]).astype(gb_ref.dtype)


def _block_diag(yb, bdmask):
    y2 = jnp.concatenate([yb, yb], axis=0)
    return jnp.where(bdmask, y2, jnp.zeros_like(y2))


def _chunk_prep(ins, consts):
    ltri, eye2, trimask, bdmask, bdmask32, lane0, ones_c = consts
    pairs = range(len(ins))
    ldc, rc, kc, vc, kknc, bc = ([x[i] for x in ins] for i in range(6))
    ld_hi = [ldc[p].astype(BF16) for p in pairs]
    ld_lo = [(ldc[p] - ld_hi[p].astype(F32)).astype(BF16) for p in pairs]
    ld2 = [jnp.concatenate([ld_hi[p], ld_lo[p]], axis=0) for p in pairs]
    cum = [_dot(ltri, ld2[p]) for p in pairs]
    wl_col = [jnp.exp(_dot_tn(ld2[p], ones_c)) for p in pairs]
    atb, rtb, ys, bw, kw = [], [], [], [], []
    for p in pairs:
        w_inv = jnp.exp(-cum[p])
        bt = (bc[p] * w_inv).astype(BF16)
        kt = (kc[p] * w_inv).astype(BF16)
        zero = jnp.zeros_like(bt)
        ys.append(jnp.concatenate([jnp.where(lane0, bt, zero), jnp.where(lane0, zero, bt),
                                   jnp.where(lane0, kt, zero), jnp.where(lane0, zero, kt)], axis=0))
        atb.append((-(kknc[p] * jnp.exp(cum[p] - ldc[p]))).astype(BF16))
        rtb.append((rc[p] * jnp.exp(cum[p])).astype(BF16))
        w_rest = jnp.exp(cum[p][CHUNK - 1:CHUNK, :] - cum[p])
        bw.append((bc[p] * w_rest).astype(BF16))
        kw.append((kc[p] * w_rest).astype(BF16))
    sc = [jnp.where(trimask, _dot_nt(jnp.concatenate([atb[p], rtb[p]], axis=0), ys[p]), 0.0) for p in pairs]
    a_ab = [sc[p][0:CHUNK, 0:PAIR] for p in pairs]
    a_akb = [sc[p][0:CHUNK, PAIR:2 * PAIR].astype(BF16) for p in pairs]
    r_bb = [sc[p][CHUNK:2 * CHUNK, 0:PAIR].astype(BF16) for p in pairs]
    r_kb = [sc[p][CHUNK:2 * CHUNK, PAIR:2 * PAIR].astype(BF16) for p in pairs]

    n = [_dot(a_ab[p].astype(BF16), _block_diag(a_ab[p].astype(BF16), bdmask)) for p in pairs]
    t = [eye2 + a_ab[p] for p in pairs]
    for _ in range(4):
        res = [_dot(jnp.concatenate([n[p], t[p]], axis=0).astype(BF16), _block_diag(n[p].astype(BF16), bdmask))
               for p in pairs]
        n = [res[p][0:CHUNK] for p in pairs]
        t = [t[p] + res[p][CHUNK:2 * CHUNK] for p in pairs]
    tb = [(t[p] + _dot(t[p].astype(BF16), _block_diag(n[p].astype(BF16), bdmask))).astype(BF16) for p in pairs]
    vb = [vc[p].astype(BF16) for p in pairs]
    v_bd = [_block_diag(vb[p], bdmask) for p in pairs]
    return [dict(tb=tb[p], atb=atb[p], rtb=rtb[p], a_akb=a_akb[p], r_bb=r_bb[p], r_kb=r_kb[p], bw=bw[p], kw=kw[p],
                 vb=vb[p], v_bd=v_bd[p], wl_col=wl_col[p]) for p in pairs]


def _chunk_apply(pre, m_bd, consts):
    bdmask, bdmask32 = consts[3], consts[4]
    pairs = range(len(pre))
    mb = [m_bd[p].astype(BF16) for p in pairs]
    q = [_dot(jnp.concatenate([pre[p]["atb"], pre[p]["a_akb"]], axis=1),
              jnp.concatenate([mb[p], pre[p]["v_bd"]], axis=0)) for p in pairs]
    ub = [_dot(pre[p]["tb"], _block_diag(q[p].astype(BF16), bdmask)).astype(BF16) for p in pairs]
    y = [_dot(jnp.concatenate([pre[p]["rtb"], pre[p]["r_bb"], pre[p]["r_kb"]], axis=1),
              jnp.concatenate([mb[p], _block_diag(ub[p], bdmask), pre[p]["v_bd"]], axis=0)) for p in pairs]
    upd = [_dot_tn(jnp.concatenate([pre[p]["bw"], pre[p]["kw"]], axis=0),
                   jnp.concatenate([ub[p], pre[p]["vb"]], axis=0)) for p in pairs]
    m_new = [pre[p]["wl_col"] * m_bd[p] + jnp.where(bdmask32, upd[p], 0.0) for p in pairs]
    return y, m_new


def _mix_core_body(r_ref, ld_ref, k2_ref, v_ref, kkn_ref, b_ref, g_ref, ycg_ref, gb_ref, h_ref,
                   ltri_ref, eye2_ref, trimask_ref, bdmask_ref, bdmask32_ref, lane0_ref, gones_ref, gmean_ref,
                   rk_ref, lnxg_ref, lnxb_ref, wro_ref, wo_ref, ln1g_ref, ln1b_ref,
                   h1_ref, m_s, o_s, ob_s, t32_s, rk_s, bon_s, *, tt, alpha):
    ti = pl.program_id(1)

    @pl.when(ti == 0)
    def _():
        m_s[...] = jnp.zeros(m_s.shape, F32)

    consts = (ltri_ref[...], eye2_ref[...], trimask_ref[...] > 0.5, bdmask_ref[...] > 0.5,
              bdmask32_ref[...] > 0.5, lane0_ref[...] > 0.5, jnp.ones((2 * CHUNK, PAIR), BF16))

    n_chunk = tt // CHUNK
    cols = [slice(p * PAIR, (p + 1) * PAIR) for p in range(N_PAIR)]
    ins = []
    for c in range(n_chunk):
        rows = slice(c * CHUNK, (c + 1) * CHUNK)
        ins += [(ld_ref[0, rows, cs], r_ref[0, rows, cs], k2_ref[0, rows, cs], v_ref[0, rows, cs],
                 kkn_ref[0, rows, cs], b_ref[0, rows, cs]) for cs in cols]
    pre = _chunk_prep(ins, consts)
    m = [m_s[p] for p in range(N_PAIR)]
    for c in range(n_chunk):
        y, m = _chunk_apply(pre[c * N_PAIR:(c + 1) * N_PAIR], m, consts)
        for p in range(N_PAIR):
            o_s[c * CHUNK:(c + 1) * CHUNK, cols[p]] = y[p]
    for p in range(N_PAIR):
        m_s[p] = m[p]

    blocks = [slice(base, base + RB) for base in range(0, tt, RB)]
    for rows in blocks:
        ob_s[rows, :] = o_s[rows, :].astype(BF16)
    t32_s[...] = _seg_sum(ob_s[...], gmean_ref)
    for rows in blocks:
        xc = o_s[rows, :] - t32_s[rows, :]
        o_s[rows, :] = xc
        ob_s[rows, :] = (xc * xc).astype(BF16)
        rk_s[rows, :] = (r_ref[0, rows, :] * k2_ref[0, rows, :] * rk_ref[...]).astype(BF16)
    t32_s[...] = _seg_sum(ob_s[...], gmean_ref)
    bon_s[...] = _seg_sum(rk_s[...], gones_ref)
    for rows in blocks:
        on = o_s[rows, :] * lax.rsqrt(t32_s[rows, :] + GN_EPS) * lnxg_ref[...] + lnxb_ref[...]
        o2 = (on + bon_s[rows, :] * v_ref[0, rows, :]) * g_ref[0, rows, :].astype(F32)
        ob_s[rows, :] = o2.astype(BF16)
    t32_s[...] = _dot(ob_s[...], wro_ref[...])
    for rows in blocks:
        y = ycg_ref[0, rows, :].astype(F32) + gb_ref[0, rows, :].astype(F32) * t32_s[rows, :]
        ob_s[rows, :] = y.astype(BF16)
    t32_s[...] = _dot(ob_s[...], wo_ref[...])
    for rows in blocks:
        h1_ref[0, rows, :] = _layer_norm(alpha * h_ref[0, rows, :] + t32_s[rows, :],
                                         ln1g_ref[...], ln1b_ref[...], LN_EPS)


def _ffn_body(h1_ref, hist_ref, wup_ref, dw_ref, wdn_ref, ln2g_ref, ln2b_ref, out_ref,
              hb_s, carry_s, y_s, *, tt, alpha):
    ti = pl.program_id(1)

    @pl.when(ti == 0)
    def _():
        carry_s[...] = _dot(hist_ref[0].astype(BF16), wup_ref[...])

    for base in range(0, tt, RB):
        rows = slice(base, base + RB)
        hb_s[rows, :] = h1_ref[0, rows, :].astype(BF16)

    n_blk = D_FF // FB

    def up(j):
        hb = hb_s[...]
        return [_dot(hb, wup_ref[:, half * D_FF + j * FB:half * D_FF + (j + 1) * FB]) for half in (0, 1)]

    def conv(cols, u):
        ext = jnp.concatenate([carry_s[:, cols], u], axis=0)
        carry_s[:, cols] = u[tt - 8:tt]
        return (dw_ref[2:3, cols] * u + dw_ref[1:2, cols] * pltpu.roll(ext, 1, axis=0)[8:8 + tt]
                + dw_ref[0:1, cols] * pltpu.roll(ext, 2, axis=0)[8:8 + tt])

    def act(j, u):
        lin = conv(slice(j * FB, (j + 1) * FB), u[0])
        gate = conv(slice(D_FF + j * FB, D_FF + (j + 1) * FB), u[1])
        return (gate * _sigmoid(gate) * lin).astype(BF16)

    u_next = up(0)
    for j in range(n_blk):
        u = u_next
        if j + 1 < n_blk:
            u_next = up(j + 1)
        d = _dot(act(j, u), wdn_ref[j * FB:(j + 1) * FB, :])
        if j == 0:
            y_s[...] = d
        else:
            y_s[...] += d

    for base in range(0, tt, RB):
        rows = slice(base, base + RB)
        out_ref[0, rows, :] = _layer_norm(alpha * h1_ref[0, rows, :] + y_s[rows, :],
                                          ln2g_ref[...], ln2b_ref[...], LN_EPS)


def _const(shape):
    nd = len(shape)
    return pl.BlockSpec(shape, lambda b, t: (0,) * nd, pipeline_mode=pl.Buffered(1))


def _row2d(v):
    return v.reshape(1, -1).astype(F32)


def _mask_constants():
    t = np.arange(2 * CHUNK)[:, None] % CHUNK
    s = np.arange(2 * PAIR)[None, :] % RWKV_HEAD
    strict = (t > s)
    incl = (t >= s)
    trimask = np.where(np.arange(2 * CHUNK)[:, None] < CHUNK, strict, incl).astype(np.float32)
    hh = np.arange(PAIR) // RWKV_HEAD
    bdmask = (hh[:, None] == hh[None, :]).astype(np.float32)
    ltri = np.concatenate([np.tril(np.ones((CHUNK, CHUNK), np.float32))] * 2, axis=1)
    eye2 = np.concatenate([np.eye(CHUNK, dtype=np.float32)] * 2, axis=1)
    lane0 = np.broadcast_to((np.arange(PAIR) < RWKV_HEAD)[None, :], (CHUNK, PAIR)).astype(np.float32)
    h4 = np.arange(256) // RWKV_HEAD
    gones = (h4[:, None] == h4[None, :]).astype(np.float32)
    return dict(ltri=jnp.asarray(ltri, BF16), eye2=jnp.asarray(eye2), trimask=jnp.asarray(trimask),
                bdmask=jnp.asarray(bdmask, BF16), bdmask32=jnp.asarray(bdmask), lane0=jnp.asarray(lane0, BF16),
                gones=jnp.asarray(gones, BF16), gmean=jnp.asarray(gones / RWKV_HEAD, BF16))


def _layer(x, prm, consts, bsz, n_tiles_real, alpha):
    d = D_MODEL
    tp = FRONT + n_tiles_real * TT
    c_end = 2 * d
    i_r = c_end
    i_x = i_r + 3 * d
    r_end = i_x + DECAY_LORA + ICLR_LORA + GATE_LORA
    w_in = prm["w_in"]
    wc = w_in[:, :c_end].astype(BF16)
    wrkv = w_in[:, i_r:i_x].astype(BF16)
    wx = jnp.pad(w_in[:, i_x:r_end], ((0, 0), (0, X_COLS - (r_end - i_x)))).astype(BF16)
    wgt = w_in[:, r_end:].astype(BF16)
    mu = prm["rwkv_mu"]
    mu_rkv = _row2d(mu[:3 * d])
    mu_x = _row2d(jnp.pad(mu[3 * d:], (0, X_COLS - (r_end - i_x))))
    wdu = jnp.pad(prm["w_decay_up"], ((0, 128 - DECAY_LORA), (0, 0))).astype(BF16)
    wau = jnp.pad(prm["a_up"], ((DECAY_LORA, 0), (0, 0))).astype(BF16)
    wgu = jnp.pad(prm["g_up"], ((0, 256 - GATE_LORA), (0, 0))).astype(BF16)

    tt = TT_PREP
    n_front = FRONT // tt
    n_t = tp // tt
    tile = pl.BlockSpec((1, tt, d), lambda b, t: (b, t, 0))
    in_specs = [
        pl.BlockSpec((1, tt, d), lambda b, t: (b, jnp.maximum(t - n_front, 0), 0)),
        pl.BlockSpec((tt, d), lambda b, t: (jnp.minimum(t, n_front - 1), 0)),
        _const((1, d)), _const((1, d)),
        _const((d, 2 * d)), _const((d, 3 * d)), _const((d, X_COLS)), _const((d, 2 * d)), _const((1, 2 * d)),
        _const((CONV_KERNEL, d)), _const((1, d)), _const((1, d)), _const((1, d)), _const((d, d)),
        _const((1, 3 * d)), _const((1, X_COLS)),
        _const((1, d)), _const((128, d)), _const((1, d)), _const((128, d)), _const((256, d)),
        _const((1, d)), _const((1, d)), _const((256, 256)),
    ]
    act = jax.ShapeDtypeStruct((bsz, tp, d), F32)
    act_bf = jax.ShapeDtypeStruct((bsz, tp, d), BF16)
    scratch = [
        pltpu.VMEM((tt, d), BF16),
        pltpu.VMEM((tt, 2 * d), F32),
        pltpu.VMEM((tt + HALO, d), F32),
        pltpu.VMEM((tt, d), F32),
        pltpu.VMEM((tt, d), BF16),
        pltpu.VMEM((tt + 8, 3 * d + X_COLS), F32),
        pltpu.VMEM((tt, 3 * d), F32),
        pltpu.VMEM((tt, 2 * d), F32),
        pltpu.VMEM((tt, 128), BF16), pltpu.VMEM((tt, 128), BF16), pltpu.VMEM((tt, 256), BF16),
        pltpu.VMEM((tt, d), F32), pltpu.VMEM((tt, d), F32), pltpu.VMEM((tt, d), BF16),
        pltpu.VMEM((tt, d), F32), pltpu.VMEM((tt, d), F32),
    ]
    outs = pl.pallas_call(
        functools.partial(_mix_prep_body, tt=tt),
        grid=(bsz, n_t), in_specs=in_specs, out_specs=[tile] * 10, out_shape=[act] * 7 + [act_bf] * 3,
        scratch_shapes=scratch, name="mix_prep",
        compiler_params=pltpu.CompilerParams(dimension_semantics=("arbitrary", "arbitrary"),
                                             vmem_limit_bytes=VMEM_LIMIT),
    )(x, prm["meta_pad"], prm["ln_in_g"], prm["ln_in_b"], wc, wrkv, wx, wgt, _row2d(prm["b_gate"]),
      prm["conv_dw"].astype(F32), _row2d(prm["conv_dw_b"]), _row2d(prm["conv_ln_g"]), _row2d(prm["conv_ln_b"]),
      prm["w_conv_out"].astype(BF16), mu_rkv, mu_x,
      _row2d(prm["w0"]), wdu, _row2d(prm["a0"]), wau, wgu, _row2d(prm["k_k"]), _row2d(prm["k_a"]),
      consts["gones"])
    h, r, ld, k2, v, kkn, b, g, ycg, gb = outs

    tt = TT
    n_t = tp // tt
    tile = pl.BlockSpec((1, tt, d), lambda b, t: (b, t, 0))
    in_specs = [tile] * 10 + [
        _const((CHUNK, 2 * CHUNK)), _const((CHUNK, PAIR)), _const((2 * CHUNK, 2 * PAIR)), _const((PAIR, PAIR)),
        _const((PAIR, PAIR)),
        _const((CHUNK, PAIR)), _const((256, 256)), _const((256, 256)),
        _const((1, d)), _const((1, d)), _const((1, d)), _const((d, d)), _const((d, d)), _const((1, d)), _const((1, d)),
    ]
    scratch = [
        pltpu.VMEM((N_PAIR, PAIR, PAIR), F32),
        pltpu.VMEM((tt, d), F32), pltpu.VMEM((tt, d), BF16), pltpu.VMEM((tt, d), F32),
        pltpu.VMEM((tt, d), BF16), pltpu.VMEM((tt, d), F32),
    ]
    h1 = pl.pallas_call(
        functools.partial(_mix_core_body, tt=tt, alpha=alpha),
        grid=(bsz, n_t), in_specs=in_specs, out_specs=tile, out_shape=act,
        scratch_shapes=scratch, name="mix_core",
        compiler_params=pltpu.CompilerParams(dimension_semantics=("arbitrary", "arbitrary"),
                                             vmem_limit_bytes=VMEM_LIMIT),
    )(r, ld, k2, v, kkn, b, g, ycg, gb, h,
      consts["ltri"], consts["eye2"], consts["trimask"], consts["bdmask"], consts["bdmask32"], consts["lane0"],
      consts["gones"], consts["gmean"],
      _row2d(prm["r_k"]), _row2d(prm["lnx_g"]), _row2d(prm["lnx_b"]),
      prm["w_rwkv_out"].astype(BF16), prm["w_o"].astype(BF16), _row2d(prm["ln1_g"]), _row2d(prm["ln1_b"]))

    n_front = FRONT // tt
    in_specs = [pl.BlockSpec((1, tt, d), lambda b, t: (b, t + n_front, 0)),
                pl.BlockSpec((1, 8, d), lambda b, t: (b, FRONT // 8 - 1, 0)),
                _const((d, 2 * D_FF)), _const((FFN_KERNEL, 2 * D_FF)), _const((D_FF, d)),
                _const((1, d)), _const((1, d))]
    scratch = [pltpu.VMEM((tt, d), BF16), pltpu.VMEM((8, 2 * D_FF), F32), pltpu.VMEM((tt, d), F32)]
    out = pl.pallas_call(
        functools.partial(_ffn_body, tt=tt, alpha=alpha),
        grid=(bsz, n_tiles_real), in_specs=in_specs,
        out_specs=pl.BlockSpec((1, tt, d), lambda b, t: (b, t, 0)),
        out_shape=jax.ShapeDtypeStruct((bsz, n_tiles_real * tt, d), F32),
        scratch_shapes=scratch, name="ffn",
        compiler_params=pltpu.CompilerParams(dimension_semantics=("arbitrary", "arbitrary"),
                                             vmem_limit_bytes=VMEM_LIMIT),
    )(h1, h1, prm["ffn_up"].astype(BF16), prm["ffn_dw"].astype(F32), prm["ffn_down"].astype(BF16),
      _row2d(prm["ln2_g"]), _row2d(prm["ln2_b"]))
    return out


def kernel(x, meta_tokens, ln_in_g, ln_in_b, w_in, b_gate, conv_dw, conv_dw_b, conv_ln_g, conv_ln_b, w_conv_out, rwkv_mu, w0, w_decay_up, a0, a_up, g_up, k_k, k_a, r_k, lnx_g, lnx_b, w_rwkv_out, w_o, ln1_g, ln1_b, ffn_up, ffn_dw, ffn_down, ln2_g, ln2_b):
    bsz, seq, d = x.shape
    depth = w_in.shape[0]
    assert d == D_MODEL and seq % TT == 0 and depth == 1, (x.shape, w_in.shape)
    alpha = (2.0 * depth) ** 0.25
    consts = _mask_constants()
    meta_pad = jnp.concatenate([jnp.zeros((FRONT - N_META, d), F32), meta_tokens.astype(F32)], axis=0)
    prm = dict(meta_pad=meta_pad, ln_in_g=_row2d(ln_in_g), ln_in_b=_row2d(ln_in_b),
               w_in=w_in[0], b_gate=b_gate[0], conv_dw=conv_dw[0], conv_dw_b=conv_dw_b[0],
               conv_ln_g=conv_ln_g[0], conv_ln_b=conv_ln_b[0], w_conv_out=w_conv_out[0], rwkv_mu=rwkv_mu[0],
               w0=w0[0], w_decay_up=w_decay_up[0], a0=a0[0], a_up=a_up[0], g_up=g_up[0], k_k=k_k[0], k_a=k_a[0],
               r_k=r_k[0], lnx_g=lnx_g[0], lnx_b=lnx_b[0], w_rwkv_out=w_rwkv_out[0], w_o=w_o[0],
               ln1_g=ln1_g[0], ln1_b=ln1_b[0], ffn_up=ffn_up[0], ffn_dw=ffn_dw[0], ffn_down=ffn_down[0],
               ln2_g=ln2_g[0], ln2_b=ln2_b[0])
    return _layer(x.astype(F32), prm, consts, bsz, seq // TT, alpha)
```
